```python
import math
import jax, jax.numpy as jnp
from jax import lax
import numpy as np

D_MODEL = 2048
BATCH = 4
SEQ = 2048
DEPTH = 2
DEC_BATCH = 128
DEC_SEQ = 8
PAST_LEN = 8192
PAGE_SIZE = 128

N_MIXERS = 2
N_A_LAYERS = (DEPTH + 1) // 2
N_B_LAYERS = DEPTH // 2

DIL_GROUPS = ((128, 1), (512, 4), (2048, 16))
N_GROUPS = 3
N_SLOTS = 16
HEAD_DIM_A = 128
QBLK = 128
A_SCALE = HEAD_DIM_A ** -0.5

N_BUCKETS = 32
MAX_DISTANCE = 2048

N_HEADS_B = 16
Q_LORA = 512
KV_LORA = 512
QK_NOPE = 128
QK_ROPE = 64
V_HEAD = 128
ROPE_BASE = 10000.0
MLA_SCALE = (QK_NOPE + QK_ROPE) ** -0.5

N_EXPERTS = 32
TOP_K = 4
D_FF = 2048
SWIGLU_ALPHA = 1.702
SWIGLU_LIMIT = 7.0
EXPERT_BLOCK = 128

LN_EPS = 1e-5
RMS_EPS = 1e-6
NEG_INF = -1e30
DEEPNORM_ALPHA = (2 * DEPTH) ** 0.25
DEEPNORM_BETA = (8 * DEPTH) ** -0.25

kernel_name = 'dilated_window_mla_moe_deepnorm_step'


def _layernorm(x, g, b):
    xf = x.astype(jnp.float32)
    mu = xf.mean(-1, keepdims=True)
    var = jnp.square(xf - mu).mean(-1, keepdims=True)
    return ((xf - mu) * lax.rsqrt(var + LN_EPS) * g.astype(jnp.float32) + b.astype(jnp.float32)).astype(x.dtype)


def _rmsnorm(x, g):
    xf = x.astype(jnp.float32)
    y = xf * lax.rsqrt(jnp.mean(jnp.square(xf), -1, keepdims=True) + RMS_EPS)
    return (y * g.astype(jnp.float32)).astype(x.dtype)


def _t5_bucket(dist):
    max_exact = N_BUCKETS // 2
    d = jnp.maximum(dist, 1).astype(jnp.float32)
    large = max_exact + (jnp.log(d / max_exact) / math.log(MAX_DISTANCE / max_exact)
                         * (N_BUCKETS - max_exact)).astype(jnp.int32)
    large = jnp.minimum(large, N_BUCKETS - 1)
    return jnp.where(dist < max_exact, dist, large)


def _tap_bias(rel_bias, g, dil, n_taps):
    b = rel_bias[_t5_bucket(jnp.arange(n_taps, dtype=jnp.int32) * dil)]
    return b[:, g * N_SLOTS:(g + 1) * N_SLOTS].T.astype(jnp.float32)


def _split_qkv_a(x, w_qkv):
    b, s, _ = x.shape
    qkv = jnp.einsum('bsd,de->bse', x, w_qkv).reshape(b, s, 3, N_GROUPS, N_SLOTS, HEAD_DIM_A)
    return qkv[:, :, 0], qkv[:, :, 1], qkv[:, :, 2]


def _dilated_prompt(q, k, v, bias_j, dil, n_taps):
    b, s, h, dh = q.shape
    L = -(-s // dil)
    Lp = -(-L // QBLK) * QBLK
    nb = Lp // QBLK

    def to_blocks(t):
        t = jnp.pad(t, ((0, 0), (0, L * dil - s), (0, 0), (0, 0)))
        t = t.reshape(b, L, dil, h, dh).transpose(0, 2, 1, 3, 4)
        t = jnp.pad(t, ((0, 0), (0, 0), (0, Lp - L), (0, 0), (0, 0)))
        return t.reshape(b, dil, nb, QBLK, h, dh)

    qb, kb, vb = to_blocks(q), to_blocks(k), to_blocks(v)
    prev = lambda t: jnp.pad(t, ((0, 0), (0, 0), (1, 0), (0, 0), (0, 0), (0, 0)))[:, :, :nb]
    kk = jnp.concatenate([prev(kb), kb], axis=3)
    vv = jnp.concatenate([prev(vb), vb], axis=3)
    qi = jnp.arange(QBLK)[:, None]
    ki = jnp.arange(2 * QBLK)[None, :]
    tap = QBLK + qi - ki
    ok = ((tap >= 0) & (tap < n_taps))[None] & ((jnp.arange(nb)[:, None, None] > 0) | (ki[None] >= QBLK))
    bias = bias_j[:, jnp.clip(tap, 0, n_taps - 1)]
    sc = jnp.einsum('brnqhd,brnkhd->brnhqk', qb, kk).astype(jnp.float32) * A_SCALE + bias[None, None, None]
    sc = jnp.where(ok[None, None, :, None], sc, NEG_INF)
    m = sc.max(-1, keepdims=True)
    p = jnp.exp(sc - m)
    l = p.sum(-1, keepdims=True)
    o = jnp.einsum('brnhqk,brnkhd->brnqhd', (p / l).astype(v.dtype), vv)
    lse = (m + jnp.log(l))[..., 0].transpose(0, 1, 2, 4, 3)

    def from_blocks(t):
        t = t.reshape(b, dil, Lp, *t.shape[4:])[:, :, :L]
        t = jnp.moveaxis(t, 1, 2)
        return t.reshape(b, L * dil, *t.shape[3:])[:, :s]

    return from_blocks(o), from_blocks(lse)


def _dilated_sample(q, kcat, vcat, bias_j, dil, n_taps):
    lb = kcat.shape[1] - q.shape[1]
    taps = jnp.arange(n_taps)

    def one(i):
        idx = lb + i - dil * taps
        ok = idx >= 0
        idx = jnp.maximum(idx, 0)
        kg = jnp.take(kcat, idx, axis=1)
        vg = jnp.take(vcat, idx, axis=1)
        qv = lax.dynamic_index_in_dim(q, i, axis=1, keepdims=False)
        sc = jnp.einsum('bhd,bjhd->bhj', qv, kg).astype(jnp.float32) * A_SCALE + bias_j[None]
        sc = jnp.where(ok[None, None], sc, NEG_INF)
        m = sc.max(-1, keepdims=True)
        p = jnp.exp(sc - m)
        l = p.sum(-1, keepdims=True)
        o = jnp.einsum('bhj,bjhd->bhd', (p / l).astype(vcat.dtype), vg)
        return o, (m + jnp.log(l))[..., 0]

    o, lse = lax.map(one, jnp.arange(q.shape[1]))
    return jnp.moveaxis(o, 0, 1), jnp.moveaxis(lse, 0, 1)


def _combine_groups(outs, lses):
    w = jax.nn.softmax(jnp.stack(lses, 0), axis=0)
    o = jnp.stack(outs, 0)
    return jnp.sum(w[..., None].astype(o.dtype) * o, axis=0)


def _mixer_a_prompt(x, rel_bias, w_qkv, w_o):
    b, s, _ = x.shape
    q, k, v = _split_qkv_a(x, w_qkv)
    outs, lses, states = [], [], []
    for g, (window, dil) in enumerate(DIL_GROUPS):
        n_taps = window // dil + 1
        o, lse = _dilated_prompt(q[:, :, g], k[:, :, g], v[:, :, g], _tap_bias(rel_bias, g, dil, n_taps), dil, n_taps)
        outs.append(o)
        lses.append(lse)
        keep = min(window, s)
        states.append(jnp.stack([k[:, s - keep:, g], v[:, s - keep:, g]], axis=2))
    o = _combine_groups(outs, lses).reshape(b, s, N_SLOTS * HEAD_DIM_A)
    return jnp.einsum('bse,ed->bsd', o, w_o), states


def _mixer_a_sample(x, bufs, rel_bias, w_qkv, w_o):
    b, t, _ = x.shape
    q, k, v = _split_qkv_a(x, w_qkv)
    outs, lses, states = [], [], []
    for g, (window, dil) in enumerate(DIL_GROUPS):
        n_taps = window // dil + 1
        buf = bufs[g]
        kcat = jnp.concatenate([buf[:, :, 0], k[:, :, g]], axis=1)
        vcat = jnp.concatenate([buf[:, :, 1], v[:, :, g]], axis=1)
        o, lse = _dilated_sample(q[:, :, g], kcat, vcat, _tap_bias(rel_bias, g, dil, n_taps), dil, n_taps)
        outs.append(o)
        lses.append(lse)
        n_rows = kcat.shape[1]
        keep = min(window, n_rows)
        states.append(jnp.stack([kcat[:, n_rows - keep:], vcat[:, n_rows - keep:]], axis=2))
    o = _combine_groups(outs, lses).reshape(b, t, N_SLOTS * HEAD_DIM_A)
    return jnp.einsum('bse,ed->bsd', o, w_o), states


def _rope_tables(pos):
    half = QK_ROPE // 2
    inv = ROPE_BASE ** (-jnp.arange(half, dtype=jnp.float32) / half)
    ang = pos.astype(jnp.float32)[:, None] * inv[None]
    return jnp.cos(ang), jnp.sin(ang)


def _rope(x, cos, sin):
    half = x.shape[-1] // 2
    x1, x2 = x[..., :half], x[..., half:]
    cos = cos.astype(x.dtype)
    sin = sin.astype(x.dtype)
    return jnp.concatenate([x1 * cos - x2 * sin, x1 * sin + x2 * cos], axis=-1)


def _mla_project(x, pos, w_dq, q_norm, w_uq, w_dkv, kv_norm, w_uk):
    b, s, _ = x.shape
    c_q = _rmsnorm(jnp.einsum('bsd,dr->bsr', x, w_dq), q_norm)
    q = jnp.einsum('bsr,re->bse', c_q, w_uq).reshape(b, s, N_HEADS_B, QK_NOPE + QK_ROPE)
    cos, sin = _rope_tables(pos)
    q_rope = _rope(q[..., QK_NOPE:], cos[:, None], sin[:, None])
    kv = jnp.einsum('bsd,de->bse', x, w_dkv)
    c_kv = _rmsnorm(kv[..., :KV_LORA], kv_norm)
    k_rope = _rope(kv[..., KV_LORA:], cos, sin)
    q_lat = jnp.einsum('bshn,chn->bshc', q[..., :QK_NOPE], w_uk)
    rows = jnp.concatenate([c_kv, k_rope], axis=-1)
    return q_lat, q_rope, rows


def _mla_out(o_lat, w_uv, w_o):
    v = jnp.einsum('bshc,chv->bshv', o_lat, w_uv)
    return jnp.einsum('bse,ed->bsd', v.reshape(*v.shape[:2], N_HEADS_B * V_HEAD), w_o)


def _mla_prompt_attn(q_lat, q_rope, rows):
    b, s, h, c = q_lat.shape
    nb = s // QBLK
    c_kv, k_r = rows[..., :KV_LORA], rows[..., KV_LORA:]
    kpos = jnp.arange(s)

    def block(args):
        i, ql, qr = args
        qpos = i * QBLK + jnp.arange(QBLK)
        sc = (jnp.einsum('bqhc,bkc->bhqk', ql, c_kv) + jnp.einsum('bqhr,bkr->bhqk', qr, k_r)).astype(jnp.float32) * MLA_SCALE
        sc = jnp.where(kpos[None, :] <= qpos[:, None], sc, NEG_INF)
        p = jax.nn.softmax(sc, axis=-1).astype(c_kv.dtype)
        return jnp.einsum('bhqk,bkc->bqhc', p, c_kv)

    ql = q_lat.reshape(b, nb, QBLK, h, c).swapaxes(0, 1)
    qr = q_rope.reshape(b, nb, QBLK, h, QK_ROPE).swapaxes(0, 1)
    o = lax.map(block, (jnp.arange(nb), ql, qr))
    return o.swapaxes(0, 1).reshape(b, s, h, c)


def _mla_sample_attn(q_lat, q_rope, rows_new, pool, page_table):
    t = q_lat.shape[1]
    n_past = page_table.shape[1] * pool.shape[1]
    mask = jnp.concatenate([jnp.ones((t, n_past), bool), jnp.tril(jnp.ones((t, t), bool))], axis=1)

    def one(args):
        ql, qr, rn, pt = args
        past = pool[pt].reshape(n_past, KV_LORA + QK_ROPE)
        kv = jnp.concatenate([past, rn], axis=0)
        c_kv, k_r = kv[:, :KV_LORA], kv[:, KV_LORA:]
        sc = (jnp.einsum('thc,kc->htk', ql, c_kv) + jnp.einsum('thr,kr->htk', qr, k_r)).astype(jnp.float32) * MLA_SCALE
        sc = jnp.where(mask[None], sc, NEG_INF)
        p = jax.nn.softmax(sc, axis=-1).astype(c_kv.dtype)
        return jnp.einsum('htk,kc->thc', p, c_kv)

    return lax.map(one, (q_lat, q_rope, rows_new, page_table))


def _moe(x, router_w, router_b, w_gate, b_gate, w_up, b_up, w_down, b_down):
    d = x.shape[-1]
    xt = x.reshape(-1, d)
    n_tok = xt.shape[0]
    logits = jnp.einsum('td,de->te', xt, router_w).astype(jnp.float32) + router_b.astype(jnp.float32)
    top_v, top_e = lax.top_k(logits, TOP_K)
    gates = jax.nn.softmax(top_v, axis=-1)
    n = n_tok * TOP_K
    flat_e = top_e.reshape(-1)
    flat_tok = jnp.arange(n, dtype=jnp.int32) // TOP_K
    order = jnp.argsort(flat_e, stable=True)
    se, stok, sg = flat_e[order], flat_tok[order], gates.reshape(-1)[order]
    counts = jnp.bincount(flat_e, length=N_EXPERTS)
    padded = (counts + EXPERT_BLOCK - 1) // EXPERT_BLOCK * EXPERT_BLOCK
    start = jnp.cumsum(counts) - counts
    pend = jnp.cumsum(padded)
    pstart = pend - padded
    dest = pstart[se] + jnp.arange(n) - start[se]
    n_blk = -(-(n + N_EXPERTS * (EXPERT_BLOCK - 1)) // EXPERT_BLOCK)
    n_rows = n_blk * EXPERT_BLOCK
    row_tok = jnp.zeros((n_rows,), jnp.int32).at[dest].set(stok)
    row_g = jnp.zeros((n_rows,), jnp.float32).at[dest].set(sg)
    blk_e = jnp.minimum(jnp.searchsorted(pend, jnp.arange(n_blk) * EXPERT_BLOCK, side='right'), N_EXPERTS - 1)
    xr = xt[row_tok].reshape(n_blk, EXPERT_BLOCK, d)

    def expert_block(args):
        e, xb = args
        g = jnp.minimum(xb @ w_gate[e] + b_gate[e], SWIGLU_LIMIT)
        u = jnp.clip(xb @ w_up[e] + b_up[e], -SWIGLU_LIMIT, SWIGLU_LIMIT)
        hdn = (u + 1.0) * (g * jax.nn.sigmoid(SWIGLU_ALPHA * g))
        return hdn @ w_down[e] + b_down[e]

    yr = lax.map(expert_block, (blk_e, xr)).reshape(n_rows, d)
    y = jnp.zeros_like(xt).at[row_tok].add(yr * row_g[:, None].astype(yr.dtype))
    return y.reshape(x.shape)


def setup_inputs(seed: int = 0) -> dict:
    key = jax.random.key(seed)
    ks = jax.random.split(key, 32)
    f32 = jnp.float32
    nrm = lambda k, shape, scale: jax.random.normal(k, shape, f32) * scale
    n_pages = PAST_LEN // PAGE_SIZE
    n_used = DEC_BATCH * n_pages
    n_phys = n_used + max(1, n_used // 4)
    row = KV_LORA + QK_ROPE
    page_table = jax.random.permutation(ks[6], n_phys)[:n_used].reshape(DEC_BATCH, n_pages).astype(jnp.int32)
    w_qkv = nrm(ks[8], (N_A_LAYERS, D_MODEL, 3, N_GROUPS, N_SLOTS, HEAD_DIM_A), D_MODEL ** -0.5)
    w_qkv = (w_qkv * jnp.array([1.0, 1.0, DEEPNORM_BETA], f32)[:, None, None, None]).reshape(N_A_LAYERS, D_MODEL, 3 * N_GROUPS * N_SLOTS * HEAD_DIM_A)
    inp = {
        'x_prompt': nrm(ks[0], (BATCH, SEQ, D_MODEL), 1.0),
        'x_sample': nrm(ks[1], (DEC_BATCH, DEC_SEQ, D_MODEL), 1.0),
        'cache_win_w128': nrm(ks[2], (N_A_LAYERS, DEC_BATCH, min(128, PAST_LEN), 2, N_SLOTS, HEAD_DIM_A), 1.0),
        'cache_win_w512': nrm(ks[3], (N_A_LAYERS, DEC_BATCH, min(512, PAST_LEN), 2, N_SLOTS, HEAD_DIM_A), 1.0),
        'cache_win_w2048': nrm(ks[4], (N_A_LAYERS, DEC_BATCH, min(2048, PAST_LEN), 2, N_SLOTS, HEAD_DIM_A), 1.0),
        'cache_mla': nrm(ks[5], (N_B_LAYERS, n_phys, PAGE_SIZE, row), 1.0),
        'page_table': page_table,
        'rel_bias': nrm(ks[7], (N_BUCKETS, N_GROUPS * N_SLOTS), 0.5),
        'w_qkv_a': w_qkv,
        'w_o_a': nrm(ks[9], (N_A_LAYERS, N_SLOTS * HEAD_DIM_A, D_MODEL), (N_SLOTS * HEAD_DIM_A) ** -0.5 * DEEPNORM_BETA),
        'w_dq': nrm(ks[10], (N_B_LAYERS, D_MODEL, Q_LORA), D_MODEL ** -0.5),
        'q_norm': 1.0 + nrm(ks[11], (N_B_LAYERS, Q_LORA), 0.02),
        'w_uq': nrm(ks[12], (N_B_LAYERS, Q_LORA, N_HEADS_B * (QK_NOPE + QK_ROPE)), Q_LORA ** -0.5),
        'w_dkv': nrm(ks[13], (N_B_LAYERS, D_MODEL, row), D_MODEL ** -0.5),
        'kv_norm': 1.0 + nrm(ks[14], (N_B_LAYERS, KV_LORA), 0.02),
        'w_uk': nrm(ks[15], (N_B_LAYERS, KV_LORA, N_HEADS_B, QK_NOPE), KV_LORA ** -0.5),
        'w_uv': nrm(ks[16], (N_B_LAYERS, KV_LORA, N_HEADS_B, V_HEAD), KV_LORA ** -0.5 * DEEPNORM_BETA),
        'w_o_b': nrm(ks[17], (N_B_LAYERS, N_HEADS_B * V_HEAD, D_MODEL), (N_HEADS_B * V_HEAD) ** -0.5 * DEEPNORM_BETA),
        'ln1_g': 1.0 + nrm(ks[18], (DEPTH, D_MODEL), 0.02),
        'ln1_b': nrm(ks[19], (DEPTH, D_MODEL), 0.02),
        'ln2_g': 1.0 + nrm(ks[20], (DEPTH, D_MODEL), 0.02),
        'ln2_b': nrm(ks[21], (DEPTH, D_MODEL), 0.02),
        'router_w': nrm(ks[22], (DEPTH, D_MODEL, N_EXPERTS), D_MODEL ** -0.5),
        'router_b': nrm(ks[23], (DEPTH, N_EXPERTS), 0.01),
        'w_gate': nrm(ks[24], (DEPTH, N_EXPERTS, D_MODEL, D_FF), D_MODEL ** -0.5),
        'b_gate': nrm(ks[25], (DEPTH, N_EXPERTS, D_FF), 0.01),
        'w_up': nrm(ks[26], (DEPTH, N_EXPERTS, D_MODEL, D_FF), D_MODEL ** -0.5),
        'b_up': nrm(ks[27], (DEPTH, N_EXPERTS, D_FF), 0.01),
        'w_down': nrm(ks[28], (DEPTH, N_EXPERTS, D_FF, D_MODEL), D_FF ** -0.5 * DEEPNORM_BETA),
        'b_down': nrm(ks[29], (DEPTH, N_EXPERTS, D_MODEL), 0.01),
    }
    return inp


def reference(x_prompt, x_sample, cache_win_w128, cache_win_w512, cache_win_w2048, cache_mla, page_table,
              rel_bias, w_qkv_a, w_o_a, w_dq, q_norm, w_uq, w_dkv, kv_norm, w_uk, w_uv, w_o_b,
              ln1_g, ln1_b, ln2_g, ln2_b, router_w, router_b, w_gate, b_gate, w_up, b_up, w_down, b_down):
    pos_p = jnp.arange(x_prompt.shape[1], dtype=jnp.int32)
    pos_s = PAST_LEN + jnp.arange(x_sample.shape[1], dtype=jnp.int32)
    win_caches = (cache_win_w128, cache_win_w512, cache_win_w2048)
    win_p = [[] for _ in DIL_GROUPS]
    win_s = [[] for _ in DIL_GROUPS]
    mla_p, mla_s = [], []
    xp, xs = x_prompt, x_sample
    for layer in range(DEPTH):
        li = layer // N_MIXERS
        if layer % N_MIXERS == 0:
            mp, st_p = _mixer_a_prompt(xp, rel_bias, w_qkv_a[li], w_o_a[li])
            ms, st_s = _mixer_a_sample(xs, [c[li] for c in win_caches], rel_bias, w_qkv_a[li], w_o_a[li])
            for g in range(N_GROUPS):
                win_p[g].append(st_p[g])
                win_s[g].append(st_s[g])
        else:
            mla_w = (w_dq[li], q_norm[li], w_uq[li], w_dkv[li], kv_norm[li], w_uk[li])
            ql, qr, rows_p = _mla_project(xp, pos_p, *mla_w)
            mp = _mla_out(_mla_prompt_attn(ql, qr, rows_p), w_uv[li], w_o_b[li])
            mla_p.append(rows_p)
            ql, qr, rows_s = _mla_project(xs, pos_s, *mla_w)
            ms = _mla_out(_mla_sample_attn(ql, qr, rows_s, cache_mla[li], page_table), w_uv[li], w_o_b[li])
            mla_s.append(rows_s)
        xp = _layernorm(DEEPNORM_ALPHA * xp + mp, ln1_g[layer], ln1_b[layer])
        xs = _layernorm(DEEPNORM_ALPHA * xs + ms, ln1_g[layer], ln1_b[layer])
        moe_w = (router_w[layer], router_b[layer], w_gate[layer], b_gate[layer],
                 w_up[layer], b_up[layer], w_down[layer], b_down[layer])
        xp = _layernorm(DEEPNORM_ALPHA * xp + _moe(xp, *moe_w), ln2_g[layer], ln2_b[layer])
        xs = _layernorm(DEEPNORM_ALPHA * xs + _moe(xs, *moe_w), ln2_g[layer], ln2_b[layer])
    win128_p = jnp.stack(win_p[0], 0)
    win512_p = jnp.stack(win_p[1], 0)
    win2048_p = jnp.stack(win_p[2], 0)
    new_mla_p = jnp.stack(mla_p, 0)
    win128_s = jnp.stack(win_s[0], 0)
    win512_s = jnp.stack(win_s[1], 0)
    win2048_s = jnp.stack(win_s[2], 0)
    new_mla_s = jnp.stack(mla_s, 0)
    return (xp, xs, win128_p, win512_p, win2048_p, new_mla_p, win128_s, win512_s, win2048_s, new_mla_s)
```

```python
import functools
import math

import jax
import jax.numpy as jnp
from jax import lax
from jax.experimental import pallas as pl
from jax.experimental.pallas import tpu as pltpu

F32 = jnp.float32
BF16 = jnp.bfloat16

DIL_GROUPS = ((128, 1), (512, 4), (2048, 16))
N_GROUPS = len(DIL_GROUPS)
N_SLOTS = 16
HEAD_DIM_A = 128
QBLK = 128
A_SCALE = HEAD_DIM_A ** -0.5
N_BUCKETS = 32
MAX_DISTANCE = 2048
N_HEADS_B = 16
KV_LORA = 512
QK_NOPE = 128
QK_ROPE = 64
V_HEAD = 128
ROPE_BASE = 10000.0
MLA_SCALE = (QK_NOPE + QK_ROPE) ** -0.5
MLA_ROW = KV_LORA + QK_ROPE
TOP_K = 4
SWIGLU_ALPHA = 1.702
SWIGLU_LIMIT = 7.0
LN_EPS = 1e-5
RMS_EPS = 1e-6
NEG_INF = -1e30

LANES = 128
SUBLANES = 8
VMEM_LIMIT = 56 * 1024 * 1024
KV_ROWS = 2 * N_SLOTS

NT_DIMS = (((1,), (1,)), ((), ()))


def _pick(n, candidates):
    for c in candidates:
        if n % c == 0:
            return c
    raise ValueError(f"no tile in {candidates} divides {n}")


def _params(*sem):
    return pltpu.CompilerParams(dimension_semantics=sem, vmem_limit_bytes=VMEM_LIMIT)


def _mm_kernel(x_ref, w_ref, o_ref, wb_ref):
    @pl.when(pl.program_id(1) == 0)
    def _():
        wb_ref[...] = w_ref[...].astype(BF16)

    o_ref[...] = jnp.dot(x_ref[...].astype(BF16), wb_ref[...],
                         preferred_element_type=F32).astype(o_ref.dtype)


def _matmul(x, w, out_dtype=F32):
    m, k = x.shape
    n = w.shape[1]
    tm = _pick(m, (512, 256, 128, 64, 32, 16, 8))
    tn = _pick(n, (512, 256, 128))
    return pl.pallas_call(
        _mm_kernel,
        grid=(n // tn, m // tm),
        in_specs=[pl.BlockSpec((tm, k), lambda j, i: (i, 0)),
                  pl.BlockSpec((k, tn), lambda j, i: (0, j))],
        out_specs=pl.BlockSpec((tm, tn), lambda j, i: (i, j)),
        out_shape=jax.ShapeDtypeStruct((m, n), out_dtype),
        scratch_shapes=[pltpu.VMEM((k, tn), BF16)],
        compiler_params=_params("arbitrary", "arbitrary"),
        name="matmul",
    )(x, w)


def _qkv_kernel(x_ref, w_ref, nat_ref, hm_ref, wb_ref, *, heads_per_tile):
    @pl.when(pl.program_id(1) == 0)
    def _():
        wb_ref[...] = w_ref[...].astype(BF16)

    acc = jnp.dot(x_ref[...].astype(BF16), wb_ref[...], preferred_element_type=F32)
    nat_ref[...] = acc
    for j in range(heads_per_tile):
        hm_ref[j] = acc[:, j * HEAD_DIM_A:(j + 1) * HEAD_DIM_A]


def _qkv_proj(x, w):
    m, k = x.shape
    n = w.shape[1]
    tm = _pick(m, (512, 256, 128, 64, 32, 16, 8))
    tn = 512
    hpt = tn // HEAD_DIM_A
    return pl.pallas_call(
        functools.partial(_qkv_kernel, heads_per_tile=hpt),
        grid=(n // tn, m // tm),
        in_specs=[pl.BlockSpec((tm, k), lambda j, i: (i, 0)),
                  pl.BlockSpec((k, tn), lambda j, i: (0, j))],
        out_specs=[pl.BlockSpec((tm, tn), lambda j, i: (i, j)),
                   pl.BlockSpec((hpt, tm, HEAD_DIM_A), lambda j, i: (j, i, 0))],
        out_shape=[jax.ShapeDtypeStruct((m, n), F32),
                   jax.ShapeDtypeStruct((n // HEAD_DIM_A, m, HEAD_DIM_A), F32)],
        scratch_shapes=[pltpu.VMEM((k, tn), BF16)],
        compiler_params=_params("arbitrary", "arbitrary"),
        name="qkv_proj",
    )(x, w)


def _add_ln_kernel(x_ref, y_ref, g_ref, b_ref, o_ref, *, alpha):
    z = alpha * x_ref[...] + y_ref[...].astype(F32)
    mu = jnp.mean(z, axis=-1, keepdims=True)
    zc = z - mu
    var = jnp.mean(zc * zc, axis=-1, keepdims=True)
    o_ref[...] = zc * lax.rsqrt(var + LN_EPS) * g_ref[...] + b_ref[...]


def _add_ln(x, y, g, b, alpha):
    m, d = x.shape
    tm = _pick(m, (512, 256, 128, 64, 32, 16, 8))
    return pl.pallas_call(
        functools.partial(_add_ln_kernel, alpha=alpha),
        grid=(m // tm,),
        in_specs=[pl.BlockSpec((tm, d), lambda i: (i, 0)),
                  pl.BlockSpec((tm, d), lambda i: (i, 0)),
                  pl.BlockSpec((1, d), lambda i: (0, 0)),
                  pl.BlockSpec((1, d), lambda i: (0, 0))],
        out_specs=pl.BlockSpec((tm, d), lambda i: (i, 0)),
        out_shape=jax.ShapeDtypeStruct((m, d), F32),
        compiler_params=_params("arbitrary"),
        name="add_ln",
    )(x, y, g.reshape(1, d), b.reshape(1, d))


def _t5_bucket(dist):
    max_exact = N_BUCKETS // 2
    d = jnp.maximum(dist, 1).astype(F32)
    large = max_exact + (jnp.log(d / max_exact) / math.log(MAX_DISTANCE / max_exact)
                         * (N_BUCKETS - max_exact)).astype(jnp.int32)
    large = jnp.minimum(large, N_BUCKETS - 1)
    return jnp.where(dist < max_exact, dist, large)


def _tap_bias(rel_bias, g, dil, n_taps):
    b = rel_bias[_t5_bucket(jnp.arange(n_taps, dtype=jnp.int32) * dil)]
    return b[:, g * N_SLOTS:(g + 1) * N_SLOTS].T.astype(F32)


def _tap_table(bias_j, tap, n_taps):
    ok = (tap >= 0) & (tap < n_taps)
    t = jnp.take(bias_j, jnp.clip(tap, 0, n_taps - 1), axis=1)
    return jnp.where(ok[None], t, NEG_INF)


def _dil_prompt_kernel(*refs, seq):
    qkv = refs[:9]
    bias = refs[9:12]
    o_ref = refs[12]
    o_scr, lse_scr = refs[13], refs[14]

    for g, (window, dil) in enumerate(DIL_GROUPS):
        q_ref, k_ref, v_ref = qkv[3 * g], qkv[3 * g + 1], qkv[3 * g + 2]
        b_ref = bias[g]
        nb = seq // dil // QBLK

        def blk(t, carry, q_ref=q_ref, k_ref=k_ref, v_ref=v_ref, b_ref=b_ref, dil=dil, nb=nb, g=g):
            r = t // nb
            n = t % nb
            start = r + n * (QBLK * dil)
            pstart = r + jnp.maximum(n - 1, 0) * (QBLK * dil)
            cur = pl.ds(start, QBLK, stride=dil) if dil > 1 else pl.ds(start, QBLK)
            prv = pl.ds(pstart, QBLK, stride=dil) if dil > 1 else pl.ds(pstart, QBLK)
            q = q_ref[cur, :].astype(BF16)
            kk = jnp.concatenate([k_ref[prv, :], k_ref[cur, :]], axis=0).astype(BF16)
            vv = jnp.concatenate([v_ref[prv, :], v_ref[cur, :]], axis=0).astype(BF16)
            s = lax.dot_general(q, kk, NT_DIMS, preferred_element_type=F32) * A_SCALE + b_ref[...]
            ki = lax.broadcasted_iota(jnp.int32, s.shape, 1)
            s = jnp.where((ki >= QBLK) | (n > 0), s, NEG_INF)
            m = jnp.max(s, axis=-1, keepdims=True)
            p = jnp.exp(s - m)
            l = jnp.sum(p, axis=-1, keepdims=True)
            o = jnp.dot(p.astype(BF16), vv, preferred_element_type=F32) / l
            o_scr[g, cur, :] = o
            lse_scr[g, cur, :] = jnp.broadcast_to(m + jnp.log(l), (QBLK, HEAD_DIM_A))
            return carry

        lax.fori_loop(0, dil * nb, blk, 0)

    rows = 256

    def merge(c, carry):
        sl = pl.ds(pl.multiple_of(c * rows, rows), rows)
        ls = [lse_scr[g, sl, :] for g in range(N_GROUPS)]
        mx = jnp.maximum(jnp.maximum(ls[0], ls[1]), ls[2])
        es = [jnp.exp(x - mx) for x in ls]
        den = es[0] + es[1] + es[2]
        out = (es[0] / den) * o_scr[0, sl, :] + (es[1] / den) * o_scr[1, sl, :] + (es[2] / den) * o_scr[2, sl, :]
        o_ref[sl, :] = out.astype(o_ref.dtype)
        return carry

    lax.fori_loop(0, seq // rows, merge, 0)


def _dil_prompt(qkv_hm, bias_mats, batch, seq):
    gh = N_GROUPS * N_SLOTS
    in_specs = []
    for g in range(N_GROUPS):
        for which in range(3):
            in_specs.append(pl.BlockSpec(
                (None, seq, HEAD_DIM_A),
                functools.partial(lambda b, h, base: (base + h, b, 0), base=which * gh + g * N_SLOTS)))
    for g in range(N_GROUPS):
        in_specs.append(pl.BlockSpec((None, QBLK, 2 * QBLK), lambda b, h: (h, 0, 0)))
    return pl.pallas_call(
        functools.partial(_dil_prompt_kernel, seq=seq),
        grid=(batch, N_SLOTS),
        in_specs=in_specs,
        out_specs=pl.BlockSpec((seq, HEAD_DIM_A), lambda b, h: (b, h)),
        out_shape=jax.ShapeDtypeStruct((batch * seq, N_SLOTS * HEAD_DIM_A), BF16),
        scratch_shapes=[pltpu.VMEM((N_GROUPS, seq, HEAD_DIM_A), F32),
                        pltpu.VMEM((N_GROUPS, seq, HEAD_DIM_A), F32)],
        compiler_params=_params("arbitrary", "arbitrary"),
        name="dil_prompt",
    )(*([qkv_hm] * 9), *bias_mats)


def _softmax_step(s, v, m_ref, l_ref, acc_ref, h):
    m_prev = m_ref[h]
    m_new = jnp.maximum(m_prev, jnp.max(s, axis=-1, keepdims=True))
    alpha = jnp.exp(m_prev - m_new)
    p = jnp.exp(s - m_new[:, :1])
    l_ref[h] = alpha * l_ref[h] + jnp.sum(p, axis=-1, keepdims=True)
    acc_ref[h] = alpha * acc_ref[h] + jnp.dot(p.astype(BF16), v, preferred_element_type=F32)
    m_ref[h] = m_new


def _win_sample_kernel(cache_ref, next_ref, new_ref, q_ref, tp_ref, tn_ref,
                       win_ref, o_ref, lse_ref, m_ref, l_ref, acc_ref, *, rc, nc, t_new):
    c = pl.program_id(1)
    shift = t_new * KV_ROWS
    keep = (rc - t_new) * KV_ROWS

    @pl.when(c == 0)
    def _():
        m_ref[...] = jnp.full(m_ref.shape, NEG_INF, F32)
        l_ref[...] = jnp.zeros(l_ref.shape, F32)
        acc_ref[...] = jnp.zeros(acc_ref.shape, F32)

    win_ref[0:keep, :] = cache_ref[shift:, :]

    @pl.when(c < nc - 1)
    def _():
        win_ref[keep:, :] = next_ref[...]

    @pl.when(c == nc - 1)
    def _():
        win_ref[keep:, :] = new_ref[...]

    for h in range(N_SLOTS):
        q = q_ref[:, h * HEAD_DIM_A:(h + 1) * HEAD_DIM_A].astype(BF16)
        k = cache_ref[pl.ds(h, rc, stride=KV_ROWS), :].astype(BF16)
        v = cache_ref[pl.ds(N_SLOTS + h, rc, stride=KV_ROWS), :].astype(BF16)
        s = lax.dot_general(q, k, NT_DIMS, preferred_element_type=F32) * A_SCALE + tp_ref[h]
        _softmax_step(s, v, m_ref, l_ref, acc_ref, h)

    @pl.when(c == nc - 1)
    def _():
        pad = jnp.zeros((LANES - t_new, HEAD_DIM_A), F32)
        for h in range(N_SLOTS):
            q = q_ref[:, h * HEAD_DIM_A:(h + 1) * HEAD_DIM_A].astype(BF16)
            k = jnp.concatenate([new_ref[pl.ds(h, t_new, stride=KV_ROWS), :], pad], axis=0).astype(BF16)
            v = jnp.concatenate([new_ref[pl.ds(N_SLOTS + h, t_new, stride=KV_ROWS), :], pad], axis=0).astype(BF16)
            s = lax.dot_general(q, k, NT_DIMS, preferred_element_type=F32) * A_SCALE + tn_ref[h]
            _softmax_step(s, v, m_ref, l_ref, acc_ref, h)
            sl = slice(h * HEAD_DIM_A, (h + 1) * HEAD_DIM_A)
            o_ref[:, sl] = acc_ref[h] / l_ref[h]
            lse_ref[:, sl] = m_ref[h] + jnp.log(l_ref[h])


def _win_sample(cache, li, newkv, qkv_nat, q_row_blk0, g, t_past, t_newtab, t_new):
    _, nb_, wrows, _ = cache.shape
    w = wrows // KV_ROWS
    rc = min(w, 256)
    nc = w // rc
    tail = t_new * KV_ROWS
    d = N_SLOTS * HEAD_DIM_A
    n_tail_blocks = wrows // tail
    kern = functools.partial(_win_sample_kernel, rc=rc, nc=nc, t_new=t_new)
    return pl.pallas_call(
        kern,
        grid=(nb_, nc),
        in_specs=[
            pl.BlockSpec((None, None, rc * KV_ROWS, LANES), lambda b, c: (li, b, c, 0)),
            pl.BlockSpec((None, None, tail, LANES),
                         lambda b, c: (li, b, jnp.minimum((c + 1) * (rc // t_new), n_tail_blocks - 1), 0)),
            pl.BlockSpec((None, tail, LANES), lambda b, c: (b, 0, 0)),
            pl.BlockSpec((t_new, d), lambda b, c: (q_row_blk0 + b, g)),
            pl.BlockSpec((N_SLOTS, t_new, rc), lambda b, c: (0, 0, c)),
            pl.BlockSpec((N_SLOTS, t_new, LANES), lambda b, c: (0, 0, 0)),
        ],
        out_specs=[
            pl.BlockSpec((None, rc * KV_ROWS, LANES), lambda b, c: (b, c, 0)),
            pl.BlockSpec((t_new, d), lambda b, c: (b, 0)),
            pl.BlockSpec((t_new, d), lambda b, c: (b, 0)),
        ],
        out_shape=[
            jax.ShapeDtypeStruct((nb_, wrows, LANES), F32),
            jax.ShapeDtypeStruct((nb_ * t_new, d), F32),
            jax.ShapeDtypeStruct((nb_ * t_new, d), F32),
        ],
        scratch_shapes=[pltpu.VMEM((N_SLOTS, t_new, LANES), F32)] * 3,
        compiler_params=_params("arbitrary", "arbitrary"),
        name=f"win_sample_w{w}",
    )(cache, cache, newkv, qkv_nat, t_past, t_newtab)


def _merge_groups_kernel(o0, o1, o2, l0, l1, l2, out_ref):
    ls = [l0[...], l1[...], l2[...]]
    mx = jnp.maximum(jnp.maximum(ls[0], ls[1]), ls[2])
    es = [jnp.exp(x - mx) for x in ls]
    den = es[0] + es[1] + es[2]
    out_ref[...] = (es[0] / den) * o0[...] + (es[1] / den) * o1[...] + (es[2] / den) * o2[...]


def _merge_groups(outs, lses):
    m, d = outs[0].shape
    tm = _pick(m, (256, 128, 64, 32, 16, 8))
    spec = pl.BlockSpec((tm, d), lambda i: (i, 0))
    return pl.pallas_call(
        _merge_groups_kernel,
        grid=(m // tm,),
        in_specs=[spec] * 6,
        out_specs=spec,
        out_shape=jax.ShapeDtypeStruct((m, d), F32),
        compiler_params=_params("arbitrary"),
        name="merge_groups",
    )(*outs, *lses)


def _rms(x, g):
    return x * lax.rsqrt(jnp.mean(x * x, axis=-1, keepdims=True) + RMS_EPS) * g


def _mla_cq_kernel(x_ref, w_ref, g_ref, o_ref, wb_ref):
    @pl.when(pl.program_id(0) == 0)
    def _():
        wb_ref[...] = w_ref[...].astype(BF16)

    acc = jnp.dot(x_ref[...].astype(BF16), wb_ref[...], preferred_element_type=F32)
    o_ref[...] = _rms(acc, g_ref[...])


def _mla_cq(x, w_dq, q_norm):
    m, k = x.shape
    n = w_dq.shape[1]
    tm = _pick(m, (512, 256, 128, 64, 32, 16, 8))
    return pl.pallas_call(
        _mla_cq_kernel,
        grid=(m // tm,),
        in_specs=[pl.BlockSpec((tm, k), lambda i: (i, 0)),
                  pl.BlockSpec((k, n), lambda i: (0, 0)),
                  pl.BlockSpec((1, n), lambda i: (0, 0))],
        out_specs=pl.BlockSpec((tm, n), lambda i: (i, 0)),
        out_shape=jax.ShapeDtypeStruct((m, n), F32),
        scratch_shapes=[pltpu.VMEM((k, n), BF16)],
        compiler_params=_params("arbitrary"),
        name="mla_cq",
    )(x, w_dq, q_norm.reshape(1, n))


def _mla_rows_kernel(x_ref, w_ref, g_ref, cos_ref, sin_ref, o_ref, wb_ref):
    @pl.when(pl.program_id(0) == 0)
    def _():
        wb_ref[...] = w_ref[...].astype(BF16)

    acc = jnp.dot(x_ref[...].astype(BF16), wb_ref[...], preferred_element_type=F32)
    o_ref[:, :KV_LORA] = _rms(acc[:, :KV_LORA], g_ref[...])
    tail = acc[:, KV_LORA:]
    kr = tail[:, :QK_ROPE] * cos_ref[:, :QK_ROPE] + tail[:, QK_ROPE:] * sin_ref[:, :QK_ROPE]
    o_ref[:, KV_LORA:] = kr


def _mla_rows(x, w_ext, kv_norm, cos_t, sin_t):
    m, k = x.shape
    n = w_ext.shape[1]
    tm = _pick(m, (512, 256, 128, 64, 32, 16, 8))
    return pl.pallas_call(
        _mla_rows_kernel,
        grid=(m // tm,),
        in_specs=[pl.BlockSpec((tm, k), lambda i: (i, 0)),
                  pl.BlockSpec((k, n), lambda i: (0, 0)),
                  pl.BlockSpec((1, KV_LORA), lambda i: (0, 0)),
                  pl.BlockSpec((tm, LANES), lambda i: (i, 0)),
                  pl.BlockSpec((tm, LANES), lambda i: (i, 0))],
        out_specs=pl.BlockSpec((tm, MLA_ROW), lambda i: (i, 0)),
        out_shape=jax.ShapeDtypeStruct((m, MLA_ROW), F32),
        scratch_shapes=[pltpu.VMEM((k, n), BF16)],
        compiler_params=_params("arbitrary"),
        name="mla_rows",
    )(x, w_ext, kv_norm.reshape(1, KV_LORA), cos_t, sin_t)


def _mla_q_kernel(cq_ref, w_ref, wuk_ref, cos_ref, sin_ref, o_ref, wb_ref, wukb_ref):
    @pl.when(pl.program_id(0) == 0)
    def _():
        wb_ref[...] = w_ref[...].astype(BF16)
        wukb_ref[...] = wuk_ref[...].astype(BF16)

    hw = N_HEADS_B * LANES
    q = jnp.dot(cq_ref[...].astype(BF16), wb_ref[...], preferred_element_type=F32)
    cos = cos_ref[:, :QK_ROPE]
    sin = sin_ref[:, :QK_ROPE]
    for h in range(N_HEADS_B):
        nope = q[:, h * LANES:(h + 1) * LANES].astype(BF16)
        wuk_h = wukb_ref[:, h * QK_NOPE:(h + 1) * QK_NOPE]
        o_ref[h, :, :KV_LORA] = lax.dot_general(nope, wuk_h, NT_DIMS, preferred_element_type=F32)
        r0 = hw + h * LANES
        o_ref[h, :, KV_LORA:] = (q[:, r0:r0 + QK_ROPE] * cos
                                 + q[:, hw + r0:hw + r0 + QK_ROPE] * sin)


def _mla_q(cq, w_uq_ext, w_uk2, cos_t, sin_t):
    m, k = cq.shape
    n = w_uq_ext.shape[1]
    tm = _pick(m, (256, 128, 64, 32, 16, 8))
    return pl.pallas_call(
        _mla_q_kernel,
        grid=(m // tm,),
        in_specs=[pl.BlockSpec((tm, k), lambda i: (i, 0)),
                  pl.BlockSpec((k, n), lambda i: (0, 0)),
                  pl.BlockSpec(w_uk2.shape, lambda i: (0, 0)),
                  pl.BlockSpec((tm, LANES), lambda i: (i, 0)),
                  pl.BlockSpec((tm, LANES), lambda i: (i, 0))],
        out_specs=pl.BlockSpec((N_HEADS_B, tm, MLA_ROW), lambda i: (0, i, 0)),
        out_shape=jax.ShapeDtypeStruct((N_HEADS_B, m, MLA_ROW), F32),
        scratch_shapes=[pltpu.VMEM((k, n), BF16), pltpu.VMEM(w_uk2.shape, BF16)],
        compiler_params=_params("arbitrary"),
        name="mla_q",
    )(cq, w_uq_ext, w_uk2, cos_t, sin_t)


MLA_KV_TILE = 512


def _mla_prompt_kernel(q_ref, rows_ref, wuv_ref, o_ref, kb_ref, m_ref, l_ref, acc_ref):
    qi = pl.program_id(1)
    nq = N_HEADS_B * QBLK

    @pl.when(qi == 0)
    def _():
        kb_ref[...] = rows_ref[...].astype(BF16)

    q = q_ref[...].reshape(nq, MLA_ROW).astype(BF16)
    m_ref[...] = jnp.full(m_ref.shape, NEG_INF, F32)
    l_ref[...] = jnp.zeros(l_ref.shape, F32)
    acc_ref[...] = jnp.zeros(acc_ref.shape, F32)
    qpos = qi * QBLK + (lax.broadcasted_iota(jnp.int32, (nq, MLA_KV_TILE), 0) & (QBLK - 1))
    kcol = lax.broadcasted_iota(jnp.int32, (nq, MLA_KV_TILE), 1)

    def kv_step(j, carry):
        k = kb_ref[pl.ds(pl.multiple_of(j * MLA_KV_TILE, MLA_KV_TILE), MLA_KV_TILE), :]
        s = lax.dot_general(q, k, NT_DIMS, preferred_element_type=F32) * MLA_SCALE
        s = jnp.where(kcol + j * MLA_KV_TILE <= qpos, s, NEG_INF)
        m_prev = m_ref[...]
        m_new = jnp.maximum(m_prev, jnp.max(s, axis=-1, keepdims=True))
        alpha = jnp.exp(m_prev - m_new)
        p = jnp.exp(s - m_new[:, :1])
        l_ref[...] = alpha * l_ref[...] + jnp.sum(p, axis=-1, keepdims=True)
        acc_ref[...] = alpha[:, :1] * acc_ref[...] + jnp.dot(p.astype(BF16), k[:, :KV_LORA],
                                                             preferred_element_type=F32)
        m_ref[...] = m_new
        return carry

    lax.fori_loop(0, (qi * QBLK) // MLA_KV_TILE + 1, kv_step, 0)

    for h in range(N_HEADS_B):
        rs = slice(h * QBLK, (h + 1) * QBLK)
        o_lat = (acc_ref[rs, :] / l_ref[rs, :1]).astype(BF16)
        o_ref[:, h * V_HEAD:(h + 1) * V_HEAD] = jnp.dot(
            o_lat, wuv_ref[:, h * V_HEAD:(h + 1) * V_HEAD], preferred_element_type=F32).astype(o_ref.dtype)


def _mla_prompt(qcat, rows, wuv_b, batch, seq):
    nqb = seq // QBLK
    nq = N_HEADS_B * QBLK
    return pl.pallas_call(
        _mla_prompt_kernel,
        grid=(batch, nqb),
        in_specs=[pl.BlockSpec((N_HEADS_B, QBLK, MLA_ROW), lambda b, i: (0, b * nqb + i, 0)),
                  pl.BlockSpec((seq, MLA_ROW), lambda b, i: (b, 0)),
                  pl.BlockSpec(wuv_b.shape, lambda b, i: (0, 0))],
        out_specs=pl.BlockSpec((QBLK, N_HEADS_B * V_HEAD), lambda b, i: (b * nqb + i, 0)),
        out_shape=jax.ShapeDtypeStruct((batch * seq, N_HEADS_B * V_HEAD), F32),
        scratch_shapes=[pltpu.VMEM((seq, MLA_ROW), BF16),
                        pltpu.VMEM((nq, LANES), F32),
                        pltpu.VMEM((nq, LANES), F32),
                        pltpu.VMEM((nq, KV_LORA), F32)],
        compiler_params=_params("arbitrary", "arbitrary"),
        name="mla_prompt",
    )(qcat, rows, wuv_b)


MLA_PAGES_PER_STEP = 16


def _mla_sample_kernel(pt_ref, *refs, t_new, n_chunks):
    del pt_ref
    npg = MLA_PAGES_PER_STEP
    q_ref = refs[0]
    pages = refs[1:1 + npg]
    new_ref, wuv_ref, o_ref, m_ref, l_ref, acc_ref = refs[1 + npg:]
    c = pl.program_id(1)
    nq = N_HEADS_B * t_new

    @pl.when(c == 0)
    def _():
        m_ref[...] = jnp.full(m_ref.shape, NEG_INF, F32)
        l_ref[...] = jnp.zeros(l_ref.shape, F32)
        acc_ref[...] = jnp.zeros(acc_ref.shape, F32)

    q = q_ref[...].reshape(nq, MLA_ROW).astype(BF16)

    def update(scores, keys):
        m_prev = m_ref[...]
        m_new = m_prev
        for s in scores:
            m_new = jnp.maximum(m_new, jnp.max(s, axis=-1, keepdims=True))
        alpha = jnp.exp(m_prev - m_new)
        l_new = alpha * l_ref[...]
        acc = alpha[:, :1] * acc_ref[...]
        for s, k in zip(scores, keys):
            p = jnp.exp(s - m_new[:, :1])
            l_new = l_new + jnp.sum(p, axis=-1, keepdims=True)
            acc = acc + jnp.dot(p.astype(BF16), k[:, :KV_LORA], preferred_element_type=F32)
        m_ref[...] = m_new
        l_ref[...] = l_new
        acc_ref[...] = acc

    keys = [p_ref[...].astype(BF16) for p_ref in pages]
    scores = [lax.dot_general(q, k, NT_DIMS, preferred_element_type=F32) * MLA_SCALE for k in keys]
    update(scores, keys)

    @pl.when(c == n_chunks - 1)
    def _():
        page = pages[0].shape[0]
        k = jnp.concatenate([new_ref[...], jnp.zeros((page - t_new, MLA_ROW), F32)], axis=0).astype(BF16)
        s = lax.dot_general(q, k, NT_DIMS, preferred_element_type=F32) * MLA_SCALE
        tq = lax.broadcasted_iota(jnp.int32, s.shape, 0) % t_new
        col = lax.broadcasted_iota(jnp.int32, s.shape, 1)
        s = jnp.where(col <= tq, s, NEG_INF)
        update([s], [k])
        for h in range(N_HEADS_B):
            rs = slice(h * t_new, (h + 1) * t_new)
            o_lat = (acc_ref[rs, :] / l_ref[rs, :1]).astype(BF16)
            o_ref[:, h * V_HEAD:(h + 1) * V_HEAD] = jnp.dot(
                o_lat, wuv_ref[:, h * V_HEAD:(h + 1) * V_HEAD], preferred_element_type=F32)


def _mla_sample(qcat, q_row_blk0, rows, pool, li, page_table, wuv_b, t_new):
    nb_, n_pages = page_table.shape
    page = pool.shape[2]
    npg = MLA_PAGES_PER_STEP
    n_chunks = n_pages // npg
    nq = N_HEADS_B * t_new
    page_specs = [
        pl.BlockSpec((None, None, page, MLA_ROW),
                     functools.partial(lambda b, c, pt, k: (li, pt[b, c * npg + k], 0, 0), k=k))
        for k in range(npg)]
    grid_spec = pltpu.PrefetchScalarGridSpec(
        num_scalar_prefetch=1,
        grid=(nb_, n_chunks),
        in_specs=[pl.BlockSpec((N_HEADS_B, t_new, MLA_ROW), lambda b, c, pt: (0, q_row_blk0 + b, 0))]
        + page_specs
        + [pl.BlockSpec((t_new, MLA_ROW), lambda b, c, pt: (q_row_blk0 + b, 0)),
           pl.BlockSpec(wuv_b.shape, lambda b, c, pt: (0, 0))],
        out_specs=pl.BlockSpec((t_new, N_HEADS_B * V_HEAD), lambda b, c, pt: (b, 0)),
        scratch_shapes=[pltpu.VMEM((nq, LANES), F32),
                        pltpu.VMEM((nq, LANES), F32),
                        pltpu.VMEM((nq, KV_LORA), F32)])
    return pl.pallas_call(
        functools.partial(_mla_sample_kernel, t_new=t_new, n_chunks=n_chunks),
        grid_spec=grid_spec,
        out_shape=jax.ShapeDtypeStruct((nb_ * t_new, N_HEADS_B * V_HEAD), F32),
        compiler_params=_params("arbitrary", "arbitrary"),
        name="mla_sample",
    )(page_table, qcat, *([pool] * npg), rows, wuv_b)


def _router_kernel(x_ref, w_ref, b_ref, idx_ref, gate_ref):
    x = x_ref[...]
    w = w_ref[...]
    xh = x.astype(BF16)
    xl = (x - xh.astype(F32)).astype(BF16)
    wh = w.astype(BF16)
    wl = (w - wh.astype(F32)).astype(BF16)
    logits = (jnp.dot(xh, wh, preferred_element_type=F32) + jnp.dot(xh, wl, preferred_element_type=F32)
              + jnp.dot(xl, wh, preferred_element_type=F32)) + b_ref[...]
    n_exp = logits.shape[-1]
    lane = lax.broadcasted_iota(jnp.int32, logits.shape, 1).astype(F32)
    out_lane = lax.broadcasted_iota(jnp.int32, idx_ref.shape, 1)
    idx_out = jnp.zeros(idx_ref.shape, F32)
    val_out = jnp.zeros(gate_ref.shape, F32)
    top0 = None
    denom = None
    vals = []
    for k in range(TOP_K):
        mx = jnp.max(logits, axis=-1, keepdims=True)
        idx = jnp.min(jnp.where(logits == mx, lane, float(n_exp)), axis=-1, keepdims=True)
        if k == 0:
            top0 = mx
        e = jnp.exp(mx - top0)
        vals.append(e)
        denom = e if denom is None else denom + e
        idx_out = jnp.where(out_lane == k, idx, idx_out)
        logits = jnp.where(lane == idx, -jnp.inf, logits)
    for k in range(TOP_K):
        val_out = jnp.where(out_lane == k, vals[k] / denom, val_out)
    idx_ref[...] = idx_out.astype(jnp.int32)
    gate_ref[...] = val_out


def _router(x, w, b):
    m, d = x.shape
    n_exp = w.shape[1]
    tm = _pick(m, (256, 128, 64, 32, 16, 8))
    return pl.pallas_call(
        _router_kernel,
        grid=(m // tm,),
        in_specs=[pl.BlockSpec((tm, d), lambda i: (i, 0)),
                  pl.BlockSpec((d, n_exp), lambda i: (0, 0)),
                  pl.BlockSpec((1, n_exp), lambda i: (0, 0))],
        out_specs=[pl.BlockSpec((tm, LANES), lambda i: (i, 0)),
                   pl.BlockSpec((tm, LANES), lambda i: (i, 0))],
        out_shape=[jax.ShapeDtypeStruct((m, LANES), jnp.int32),
                   jax.ShapeDtypeStruct((m, LANES), F32)],
        compiler_params=_params("arbitrary"),
        name="router",
    )(x, w, b.reshape(1, n_exp))


MOE_TM = 256
MOE_TF = 512
MOE_TN = 512


def _moe_up_kernel(te_ref, x_ref, wg_ref, wu_ref, bg_ref, bu_ref, h_ref, wgb_ref, wub_ref):
    t = pl.program_id(1)
    changed = jnp.logical_or(t == 0, te_ref[t] != te_ref[jnp.maximum(t - 1, 0)])

    @pl.when(changed)
    def _():
        wgb_ref[...] = wg_ref[...].astype(BF16)
        wub_ref[...] = wu_ref[...].astype(BF16)

    x = x_ref[...]
    g = jnp.minimum(jnp.dot(x, wgb_ref[...], preferred_element_type=F32) + bg_ref[...], SWIGLU_LIMIT)
    u = jnp.clip(jnp.dot(x, wub_ref[...], preferred_element_type=F32) + bu_ref[...], -SWIGLU_LIMIT, SWIGLU_LIMIT)
    h_ref[...] = ((u + 1.0) * (g * (1.0 / (1.0 + jnp.exp(-SWIGLU_ALPHA * g))))).astype(h_ref.dtype)


def _moe_up(tile_e, xs, w_gate, w_up, b_gate, b_up, layer):
    n_rows, d = xs.shape
    n_exp, dff = b_gate.shape[1:]
    n_tiles = n_rows // MOE_TM
    tf = _pick(dff, (MOE_TF, 256, 128))
    grid_spec = pltpu.PrefetchScalarGridSpec(
        num_scalar_prefetch=1,
        grid=(dff // tf, n_tiles),
        in_specs=[pl.BlockSpec((MOE_TM, d), lambda f, t, te: (t, 0)),
                  pl.BlockSpec((None, None, d, tf), lambda f, t, te: (layer, te[t], 0, f)),
                  pl.BlockSpec((None, None, d, tf), lambda f, t, te: (layer, te[t], 0, f)),
                  pl.BlockSpec((None, None, 1, tf), lambda f, t, te: (layer, te[t], 0, f)),
                  pl.BlockSpec((None, None, 1, tf), lambda f, t, te: (layer, te[t], 0, f))],
        out_specs=pl.BlockSpec((MOE_TM, tf), lambda f, t, te: (t, f)),
        scratch_shapes=[pltpu.VMEM((d, tf), BF16), pltpu.VMEM((d, tf), BF16)])
    return pl.pallas_call(
        _moe_up_kernel,
        grid_spec=grid_spec,
        out_shape=jax.ShapeDtypeStruct((n_rows, dff), BF16),
        compiler_params=_params("arbitrary", "arbitrary"),
        name="moe_up",
    )(tile_e, xs, w_gate, w_up, b_gate.reshape(-1, n_exp, 1, dff), b_up.reshape(-1, n_exp, 1, dff))


def _moe_down_kernel(te_ref, h_ref, wd_ref, bd_ref, gate_ref, y_ref, wdb_ref):
    t = pl.program_id(1)
    changed = jnp.logical_or(t == 0, te_ref[t] != te_ref[jnp.maximum(t - 1, 0)])

    @pl.when(changed)
    def _():
        wdb_ref[...] = wd_ref[...].astype(BF16)

    y = jnp.dot(h_ref[...], wdb_ref[...], preferred_element_type=F32) + bd_ref[...]
    y_ref[...] = y * gate_ref[...]


def _moe_down(tile_e, hid, w_down, b_down, row_gate, layer):
    n_rows, dff = hid.shape
    n_exp, d = b_down.shape[1:]
    n_tiles = n_rows // MOE_TM
    tn = _pick(d, (MOE_TN, 256, 128))
    grid_spec = pltpu.PrefetchScalarGridSpec(
        num_scalar_prefetch=1,
        grid=(d // tn, n_tiles),
        in_specs=[pl.BlockSpec((MOE_TM, dff), lambda n, t, te: (t, 0)),
                  pl.BlockSpec((None, None, dff, tn), lambda n, t, te: (layer, te[t], 0, n)),
                  pl.BlockSpec((None, None, 1, tn), lambda n, t, te: (layer, te[t], 0, n)),
                  pl.BlockSpec((MOE_TM, 1), lambda n, t, te: (t, 0))],
        out_specs=pl.BlockSpec((MOE_TM, tn), lambda n, t, te: (t, n)),
        scratch_shapes=[pltpu.VMEM((dff, tn), BF16)])
    return pl.pallas_call(
        _moe_down_kernel,
        grid_spec=grid_spec,
        out_shape=jax.ShapeDtypeStruct((n_rows, d), F32),
        compiler_params=_params("arbitrary", "arbitrary"),
        name="moe_down",
    )(tile_e, hid, w_down, b_down.reshape(-1, n_exp, 1, d), row_gate.reshape(n_rows, 1))


def _moe(x, layer, router_w, router_b, w_gate, b_gate, w_up, b_up, w_down, b_down):
    n_tok, d = x.shape
    n_exp = router_w.shape[-1]
    idx_pad, gate_pad = _router(x, router_w[layer], router_b[layer])
    top_e = idx_pad[:, :TOP_K]
    gates = gate_pad[:, :TOP_K]
    mask = jnp.zeros((n_tok, n_exp), jnp.int32).at[jnp.arange(n_tok)[:, None], top_e].set(1)
    pos = jnp.cumsum(mask, axis=0) - mask
    counts = jnp.sum(mask, axis=0)
    padded = (counts + MOE_TM - 1) // MOE_TM * MOE_TM
    pend = jnp.cumsum(padded)
    pstart = pend - padded
    dest = pstart[top_e] + jnp.take_along_axis(pos, top_e, axis=1)
    n_tiles = -(-(n_tok * TOP_K + n_exp * (MOE_TM - 1)) // MOE_TM)
    n_rows = n_tiles * MOE_TM
    tok = jnp.broadcast_to(jnp.arange(n_tok, dtype=jnp.int32)[:, None], dest.shape)
    row_tok = jnp.zeros((n_rows,), jnp.int32).at[dest.reshape(-1)].set(tok.reshape(-1))
    row_gate = jnp.zeros((n_rows,), F32).at[dest.reshape(-1)].set(gates.reshape(-1))
    tile_e = jnp.minimum(jnp.searchsorted(pend, jnp.arange(n_tiles, dtype=jnp.int32) * MOE_TM, side='right'),
                         n_exp - 1).astype(jnp.int32)
    xs = jnp.take(x.astype(BF16), row_tok, axis=0)
    hid = _moe_up(tile_e, xs, w_gate, w_up, b_gate, b_up, layer)
    yr = _moe_down(tile_e, hid, w_down, b_down, row_gate, layer)
    y = jnp.take(yr, dest.reshape(-1), axis=0).reshape(n_tok, TOP_K, d)
    return jnp.sum(y, axis=1)


def _rope_tables(pos):
    half = QK_ROPE // 2
    inv = ROPE_BASE ** (-jnp.arange(half, dtype=F32) / half)
    ang = pos.astype(F32)[:, None] * inv[None]
    cos, sin = jnp.cos(ang), jnp.sin(ang)
    pad = jnp.zeros((pos.shape[0], LANES - QK_ROPE), F32)
    return (jnp.concatenate([cos, cos, pad], axis=1), jnp.concatenate([sin, sin, pad], axis=1))


def _rot_cols(w):
    half = w.shape[-1] // 2
    return jnp.concatenate([-w[..., half:], w[..., :half]], axis=-1)


def kernel(x_prompt, x_sample, cache_win_w128, cache_win_w512, cache_win_w2048, cache_mla, page_table, rel_bias, w_qkv_a, w_o_a, w_dq, q_norm, w_uq, w_dkv, kv_norm, w_uk, w_uv, w_o_b, ln1_g, ln1_b, ln2_g, ln2_b, router_w, router_b, w_gate, b_gate, w_up, b_up, w_down, b_down):
    batch, seq, d_model = x_prompt.shape
    dec_batch, t_new, _ = x_sample.shape
    depth = ln1_g.shape[0]
    alpha = (2 * depth) ** 0.25
    n_p = batch * seq
    n_s = dec_batch * t_new
    past_len = page_table.shape[1] * cache_mla.shape[2]
    assert t_new == SUBLANES and seq % (QBLK * DIL_GROUPS[-1][1]) == 0 and n_p % t_new == 0
    s_blk0 = n_p // t_new
    win_caches = (cache_win_w128, cache_win_w512, cache_win_w2048)
    hd = N_SLOTS * HEAD_DIM_A

    x = jnp.concatenate([x_prompt.reshape(n_p, d_model), x_sample.reshape(n_s, d_model)], axis=0)
    win_p = [[] for _ in DIL_GROUPS]
    win_s = [[] for _ in DIL_GROUPS]
    mla_p, mla_s = [], []

    for layer in range(depth):
        li = layer // 2
        if layer % 2 == 0:
            qkv_nat, qkv_hm = _qkv_proj(x, w_qkv_a[li])
            bias_mats = []
            qi = jnp.arange(QBLK)[:, None]
            ki = jnp.arange(2 * QBLK)[None, :]
            for g, (window, dil) in enumerate(DIL_GROUPS):
                n_taps = window // dil + 1
                bias_j = _tap_bias(rel_bias, g, dil, n_taps)
                bias_mats.append(_tap_table(bias_j, QBLK + qi - ki, n_taps))
            o_p = _dil_prompt(qkv_hm, bias_mats, batch, seq)
            qkv6 = qkv_nat.reshape(n_p + n_s, 3, N_GROUPS, N_SLOTS, HEAD_DIM_A)
            for g, (window, dil) in enumerate(DIL_GROUPS):
                keep = min(window, seq)
                kv = qkv6[:n_p].reshape(batch, seq, 3, N_GROUPS, N_SLOTS, HEAD_DIM_A)[:, seq - keep:, 1:, g]
                win_p[g].append(kv)
            outs, lses = [], []
            for g, (window, dil) in enumerate(DIL_GROUPS):
                n_taps = window // dil + 1
                cache = win_caches[g]
                lb = cache.shape[2]
                assert lb == window and lb % t_new == 0
                bias_j = _tap_bias(rel_bias, g, dil, n_taps)
                i_q = jnp.arange(t_new)[:, None]
                dist_past = lb + i_q - jnp.arange(lb)[None, :]
                dist_new = i_q - jnp.arange(LANES)[None, :]
                dist_new = jnp.where(jnp.arange(LANES)[None, :] < t_new, dist_new, -1)

                def table(dist):
                    tap = jnp.where(dist % dil == 0, dist // dil, -1)
                    return _tap_table(bias_j, tap, n_taps)

                newkv = qkv6[n_p:, 1:, g].reshape(dec_batch, t_new * KV_ROWS, HEAD_DIM_A)
                cache4 = cache.reshape(cache.shape[0], dec_batch, lb * KV_ROWS, HEAD_DIM_A)
                win_out, o_g, lse_g = _win_sample(cache4, li, newkv, qkv_nat, s_blk0, g,
                                                  table(dist_past), table(dist_new), t_new)
                win_s[g].append(win_out.reshape(dec_batch, lb, 2, N_SLOTS, HEAD_DIM_A))
                outs.append(o_g)
                lses.append(lse_g)
            o_s = _merge_groups(outs, lses)
            o_all = jnp.concatenate([o_p, o_s.astype(BF16)], axis=0)
            mix = _matmul(o_all, w_o_a[li])
        else:
            pos = jnp.concatenate([jnp.tile(jnp.arange(seq, dtype=jnp.int32), batch),
                                   jnp.tile(past_len + jnp.arange(t_new, dtype=jnp.int32), dec_batch)])
            cos_t, sin_t = _rope_tables(pos)
            w_kv = w_dkv[li]
            w_kv_ext = jnp.concatenate([w_kv, _rot_cols(w_kv[:, KV_LORA:])], axis=1)
            rows = _mla_rows(x, w_kv_ext, kv_norm[li], cos_t, sin_t)
            mla_p.append(rows[:n_p].reshape(batch, seq, MLA_ROW))
            mla_s.append(rows[n_p:].reshape(dec_batch, t_new, MLA_ROW))
            cq = _mla_cq(x, w_dq[li], q_norm[li])
            wq = w_uq[li].reshape(-1, N_HEADS_B, QK_NOPE + QK_ROPE)
            zpad = jnp.zeros(wq.shape[:2] + (LANES - QK_ROPE,), wq.dtype)
            w_rope = wq[..., QK_NOPE:]
            w_uq_ext = jnp.concatenate([
                wq[..., :QK_NOPE].reshape(wq.shape[0], -1),
                jnp.concatenate([w_rope, zpad], axis=-1).reshape(wq.shape[0], -1),
                jnp.concatenate([_rot_cols(w_rope), zpad], axis=-1).reshape(wq.shape[0], -1)], axis=1)
            qcat = _mla_q(cq, w_uq_ext, w_uk[li].reshape(KV_LORA, N_HEADS_B * QK_NOPE), cos_t, sin_t)
            wuv_b = w_uv[li].reshape(KV_LORA, N_HEADS_B * V_HEAD).astype(BF16)
            v_p = _mla_prompt(qcat, rows, wuv_b, batch, seq)
            v_s = _mla_sample(qcat, s_blk0, rows, cache_mla, li, page_table, wuv_b, t_new)
            mix = _matmul(jnp.concatenate([v_p, v_s], axis=0), w_o_b[li])
        x = _add_ln(x, mix, ln1_g[layer], ln1_b[layer], alpha)
        y = _moe(x, layer, router_w, router_b, w_gate, b_gate, w_up, b_up, w_down, b_down)
        x = _add_ln(x, y, ln2_g[layer], ln2_b[layer], alpha)

    xp = x[:n_p].reshape(batch, seq, d_model)
    xs = x[n_p:].reshape(dec_batch, t_new, d_model)
    return (xp, xs,
            jnp.stack(win_p[0], 0), jnp.stack(win_p[1], 0), jnp.stack(win_p[2], 0), jnp.stack(mla_p, 0),
            jnp.stack(win_s[0], 0), jnp.stack(win_s[1], 0), jnp.stack(win_s[2], 0), jnp.stack(mla_s, 0))
```

```python
import functools
import math

import jax
import jax.numpy as jnp
from jax import lax
from jax.experimental import pallas as pl
from jax.experimental.pallas import tpu as pltpu

F32 = jnp.float32
BF16 = jnp.bfloat16

DIL_GROUPS = ((128, 1), (512, 4), (2048, 16))
N_GROUPS = len(DIL_GROUPS)
N_SLOTS = 16
HEAD_DIM_A = 128
QBLK = 128
A_SCALE = HEAD_DIM_A ** -0.5
N_BUCKETS = 32
MAX_DISTANCE = 2048
N_HEADS_B = 16
KV_LORA = 512
QK_NOPE = 128
QK_ROPE = 64
V_HEAD = 128
ROPE_BASE = 10000.0
MLA_SCALE = (QK_NOPE + QK_ROPE) ** -0.5
MLA_ROW = KV_LORA + QK_ROPE
TOP_K = 4
SWIGLU_ALPHA = 1.702
SWIGLU_LIMIT = 7.0
LN_EPS = 1e-5
RMS_EPS = 1e-6
NEG_INF = -1e30

LANES = 128
SUBLANES = 8
VMEM_LIMIT = 56 * 1024 * 1024
KV_ROWS = 2 * N_SLOTS
QKV_TN = 1024

NT_DIMS = (((1,), (1,)), ((), ()))


def _pick(n, candidates):
    for c in candidates:
        if n % c == 0:
            return c
    raise ValueError(f"no tile in {candidates} divides {n}")


def _params(*sem):
    return pltpu.CompilerParams(dimension_semantics=sem, vmem_limit_bytes=VMEM_LIMIT)


def _mm_kernel(x_ref, w_ref, o_ref, wb_ref):
    @pl.when(pl.program_id(1) == 0)
    def _():
        wb_ref[...] = w_ref[...].astype(BF16)

    o_ref[...] = jnp.dot(x_ref[...].astype(BF16), wb_ref[...],
                         preferred_element_type=F32).astype(o_ref.dtype)


def _matmul(x, w, out_dtype=F32):
    m, k = x.shape
    n = w.shape[1]
    tm = _pick(m, (512, 256, 128, 64, 32, 16, 8))
    tn = _pick(n, (512, 256, 128))
    return pl.pallas_call(
        _mm_kernel,
        grid=(n // tn, m // tm),
        in_specs=[pl.BlockSpec((tm, k), lambda j, i: (i, 0)),
                  pl.BlockSpec((k, tn), lambda j, i: (0, j))],
        out_specs=pl.BlockSpec((tm, tn), lambda j, i: (i, j)),
        out_shape=jax.ShapeDtypeStruct((m, n), out_dtype),
        scratch_shapes=[pltpu.VMEM((k, tn), BF16)],
        compiler_params=_params("arbitrary", "arbitrary"),
        name="matmul",
    )(x, w)


def _qkv_kernel(x_ref, w_ref, hm_ref, wb_ref, *, heads_per_tile):
    @pl.when(pl.program_id(1) == 0)
    def _():
        wb_ref[...] = w_ref[...].astype(BF16)

    acc = jnp.dot(x_ref[...].astype(BF16), wb_ref[...], preferred_element_type=F32)
    for j in range(heads_per_tile):
        hm_ref[j] = acc[:, j * HEAD_DIM_A:(j + 1) * HEAD_DIM_A]


def _qkv_proj(x, w):
    m, k = x.shape
    n = w.shape[1]
    tm = _pick(m, (512, 256, 128, 64, 32, 16, 8))
    tn = QKV_TN
    hpt = tn // HEAD_DIM_A
    return pl.pallas_call(
        functools.partial(_qkv_kernel, heads_per_tile=hpt),
        grid=(n // tn, m // tm),
        in_specs=[pl.BlockSpec((tm, k), lambda j, i: (i, 0)),
                  pl.BlockSpec((k, tn), lambda j, i: (0, j))],
        out_specs=pl.BlockSpec((hpt, tm, HEAD_DIM_A), lambda j, i: (j, i, 0)),
        out_shape=jax.ShapeDtypeStruct((n // HEAD_DIM_A, m, HEAD_DIM_A), F32),
        scratch_shapes=[pltpu.VMEM((k, tn), BF16)],
        compiler_params=_params("arbitrary", "arbitrary"),
        name="qkv_proj",
    )(x, w)


def _win_states_kernel(k_ref, v_ref, o_ref, *, rows):
    for h in range(N_SLOTS):
        o_ref[pl.ds(h, rows, stride=KV_ROWS), :] = k_ref[h]
        o_ref[pl.ds(N_SLOTS + h, rows, stride=KV_ROWS), :] = v_ref[h]


def _win_states(qkv_hm, g, batch, seq, keep):
    rows = min(keep, 256)
    nck = keep // rows
    first = (seq - keep) // rows
    per_seq = seq // rows
    kblk = N_GROUPS + g
    vblk = 2 * N_GROUPS + g
    return pl.pallas_call(
        functools.partial(_win_states_kernel, rows=rows),
        grid=(batch, nck),
        in_specs=[pl.BlockSpec((N_SLOTS, rows, HEAD_DIM_A), lambda b, c: (kblk, b * per_seq + first + c, 0)),
                  pl.BlockSpec((N_SLOTS, rows, HEAD_DIM_A), lambda b, c: (vblk, b * per_seq + first + c, 0))],
        out_specs=pl.BlockSpec((None, rows * KV_ROWS, LANES), lambda b, c: (b, c, 0)),
        out_shape=jax.ShapeDtypeStruct((batch, keep * KV_ROWS, LANES), F32),
        compiler_params=_params("arbitrary", "arbitrary"),
        name=f"win_states_{keep}",
    )(qkv_hm, qkv_hm)


def _add_ln_kernel(x_ref, y_ref, g_ref, b_ref, o_ref, *, alpha):
    z = alpha * x_ref[...] + y_ref[...].astype(F32)
    mu = jnp.mean(z, axis=-1, keepdims=True)
    zc = z - mu
    var = jnp.mean(zc * zc, axis=-1, keepdims=True)
    o_ref[...] = zc * lax.rsqrt(var + LN_EPS) * g_ref[...] + b_ref[...]


def _add_ln(x, y, g, b, alpha):
    m, d = x.shape
    tm = _pick(m, (512, 256, 128, 64, 32, 16, 8))
    return pl.pallas_call(
        functools.partial(_add_ln_kernel, alpha=alpha),
        grid=(m // tm,),
        in_specs=[pl.BlockSpec((tm, d), lambda i: (i, 0)),
                  pl.BlockSpec((tm, d), lambda i: (i, 0)),
                  pl.BlockSpec((1, d), lambda i: (0, 0)),
                  pl.BlockSpec((1, d), lambda i: (0, 0))],
        out_specs=pl.BlockSpec((tm, d), lambda i: (i, 0)),
        out_shape=jax.ShapeDtypeStruct((m, d), F32),
        compiler_params=_params("arbitrary"),
        name="add_ln",
    )(x, y, g.reshape(1, d), b.reshape(1, d))


def _t5_bucket(dist):
    max_exact = N_BUCKETS // 2
    d = jnp.maximum(dist, 1).astype(F32)
    large = max_exact + (jnp.log(d / max_exact) / math.log(MAX_DISTANCE / max_exact)
                         * (N_BUCKETS - max_exact)).astype(jnp.int32)
    large = jnp.minimum(large, N_BUCKETS - 1)
    return jnp.where(dist < max_exact, dist, large)


def _tap_bias(rel_bias, g, dil, n_taps):
    b = rel_bias[_t5_bucket(jnp.arange(n_taps, dtype=jnp.int32) * dil)]
    return b[:, g * N_SLOTS:(g + 1) * N_SLOTS].T.astype(F32)


def _tap_table(bias_j, tap, n_taps):
    ok = (tap >= 0) & (tap < n_taps)
    t = jnp.take(bias_j, jnp.clip(tap, 0, n_taps - 1), axis=1)
    return jnp.where(ok[None], t, NEG_INF)


def _dil_prompt_kernel(*refs, seq):
    qkv = refs[:9]
    bias = refs[9:12]
    o_ref = refs[12]
    o_scr, lse_scr = refs[13], refs[14]

    for g, (window, dil) in enumerate(DIL_GROUPS):
        q_ref, k_ref, v_ref = qkv[3 * g], qkv[3 * g + 1], qkv[3 * g + 2]
        b_ref = bias[g]
        nb = seq // dil // QBLK

        def blk(t, carry, q_ref=q_ref, k_ref=k_ref, v_ref=v_ref, b_ref=b_ref, dil=dil, nb=nb, g=g):
            r = t // nb
            n = t % nb
            start = r + n * (QBLK * dil)
            pstart = r + jnp.maximum(n - 1, 0) * (QBLK * dil)
            cur = pl.ds(start, QBLK, stride=dil) if dil > 1 else pl.ds(start, QBLK)
            prv = pl.ds(pstart, QBLK, stride=dil) if dil > 1 else pl.ds(pstart, QBLK)
            q = q_ref[cur, :].astype(BF16)
            kk = jnp.concatenate([k_ref[prv, :], k_ref[cur, :]], axis=0).astype(BF16)
            vv = jnp.concatenate([v_ref[prv, :], v_ref[cur, :]], axis=0).astype(BF16)
            s = lax.dot_general(q, kk, NT_DIMS, preferred_element_type=F32) * A_SCALE + b_ref[...]
            ki = lax.broadcasted_iota(jnp.int32, s.shape, 1)
            s = jnp.where((ki >= QBLK) | (n > 0), s, NEG_INF)
            m = jnp.max(s, axis=-1, keepdims=True)
            p = jnp.exp(s - m)
            l = jnp.sum(p, axis=-1, keepdims=True)
            o = jnp.dot(p.astype(BF16), vv, preferred_element_type=F32) / l
            o_scr[g, cur, :] = o
            lse_scr[g, cur, :] = jnp.broadcast_to(m + jnp.log(l), (QBLK, HEAD_DIM_A))
            return carry

        lax.fori_loop(0, dil * nb, blk, 0)

    rows = 256

    def merge(c, carry):
        sl = pl.ds(pl.multiple_of(c * rows, rows), rows)
        ls = [lse_scr[g, sl, :] for g in range(N_GROUPS)]
        mx = jnp.maximum(jnp.maximum(ls[0], ls[1]), ls[2])
        es = [jnp.exp(x - mx) for x in ls]
        den = es[0] + es[1] + es[2]
        out = (es[0] / den) * o_scr[0, sl, :] + (es[1] / den) * o_scr[1, sl, :] + (es[2] / den) * o_scr[2, sl, :]
        o_ref[sl, :] = out.astype(o_ref.dtype)
        return carry

    lax.fori_loop(0, seq // rows, merge, 0)


def _dil_prompt(qkv_hm, bias_mats, batch, seq):
    gh = N_GROUPS * N_SLOTS
    in_specs = []
    for g in range(N_GROUPS):
        for which in range(3):
            in_specs.append(pl.BlockSpec(
                (None, seq, HEAD_DIM_A),
                functools.partial(lambda b, h, base: (base + h, b, 0), base=which * gh + g * N_SLOTS)))
    for g in range(N_GROUPS):
        in_specs.append(pl.BlockSpec((None, QBLK, 2 * QBLK), lambda b, h: (h, 0, 0)))
    return pl.pallas_call(
        functools.partial(_dil_prompt_kernel, seq=seq),
        grid=(batch, N_SLOTS),
        in_specs=in_specs,
        out_specs=pl.BlockSpec((seq, HEAD_DIM_A), lambda b, h: (b, h)),
        out_shape=jax.ShapeDtypeStruct((batch * seq, N_SLOTS * HEAD_DIM_A), BF16),
        scratch_shapes=[pltpu.VMEM((N_GROUPS, seq, HEAD_DIM_A), F32),
                        pltpu.VMEM((N_GROUPS, seq, HEAD_DIM_A), F32)],
        compiler_params=_params("arbitrary", "arbitrary"),
        name="dil_prompt",
    )(*([qkv_hm] * 9), *bias_mats)


def _win_sample_kernel(cache_ref, next_ref, q_ref, kn_ref, vn_ref, bt_hbm, btn_ref,
                       win_ref, o_ref, lse_ref, bt_ref, sem, m_ref, l_ref, acc_ref, *, rc, nc, t_new):
    b = pl.program_id(0)
    c = pl.program_id(1)
    keep = (rc - t_new) * KV_ROWS
    nq = N_SLOTS * t_new

    @pl.when((b == 0) & (c == 0))
    def _():
        cp = pltpu.make_async_copy(bt_hbm, bt_ref, sem)
        cp.start()
        cp.wait()

    @pl.when(c == 0)
    def _():
        m_ref[...] = jnp.full(m_ref.shape, NEG_INF, F32)
        l_ref[...] = jnp.zeros(l_ref.shape, F32)
        acc_ref[...] = jnp.zeros(acc_ref.shape, F32)

    win_ref[0:keep, :] = cache_ref[t_new:].reshape(keep, LANES)

    @pl.when(c < nc - 1)
    def _():
        win_ref[keep:, :] = next_ref[...].reshape(t_new * KV_ROWS, LANES)

    q = q_ref[...].reshape(nq, HEAD_DIM_A).astype(BF16)

    def step(k, v, bias):
        s = lax.dot_general(q, k, NT_DIMS, preferred_element_type=F32) * A_SCALE + bias
        m_prev = m_ref[...]
        m_new = jnp.maximum(m_prev, jnp.max(s, axis=-1, keepdims=True))
        alpha = jnp.exp(m_prev - m_new)
        p = jnp.exp(s - m_new[:, :1])
        l_ref[...] = alpha * l_ref[...] + jnp.sum(p, axis=-1, keepdims=True)
        acc_ref[...] = alpha * acc_ref[...] + jnp.dot(p.astype(BF16), v, preferred_element_type=F32)
        m_ref[...] = m_new

    step(cache_ref[:, 0].reshape(rc * N_SLOTS, HEAD_DIM_A).astype(BF16),
         cache_ref[:, 1].reshape(rc * N_SLOTS, HEAD_DIM_A).astype(BF16), bt_ref[c])

    @pl.when(c == nc - 1)
    def _():
        for h in range(N_SLOTS):
            win_ref[pl.ds(keep + h, t_new, stride=KV_ROWS), :] = kn_ref[h]
            win_ref[pl.ds(keep + N_SLOTS + h, t_new, stride=KV_ROWS), :] = vn_ref[h]
        step(kn_ref[...].reshape(nq, HEAD_DIM_A).astype(BF16),
             vn_ref[...].reshape(nq, HEAD_DIM_A).astype(BF16), btn_ref[...])
        o = acc_ref[...] / l_ref[...]
        lse = m_ref[...] + jnp.log(l_ref[...])
        for h in range(N_SLOTS):
            sl = slice(h * HEAD_DIM_A, (h + 1) * HEAD_DIM_A)
            o_ref[:, sl] = o[h * t_new:(h + 1) * t_new]
            lse_ref[:, sl] = lse[h * t_new:(h + 1) * t_new]


def _win_sample(cache, li, qkv_hm, q_row_blk0, g, bias_past, bias_new, t_new):
    nb_, w = cache.shape[1], cache.shape[2]
    nc, nq, ncol = bias_past.shape
    rc = w // nc
    d = N_SLOTS * HEAD_DIM_A
    kern = functools.partial(_win_sample_kernel, rc=rc, nc=nc, t_new=t_new)
    head_blk = (N_SLOTS, t_new, HEAD_DIM_A)
    return pl.pallas_call(
        kern,
        grid=(nb_, nc),
        in_specs=[
            pl.BlockSpec((None, None, rc, 2, N_SLOTS, HEAD_DIM_A), lambda b, c: (li, b, c, 0, 0, 0)),
            pl.BlockSpec((None, None, t_new, 2, N_SLOTS, HEAD_DIM_A),
                         lambda b, c: (li, b, jnp.minimum((c + 1) * (rc // t_new), w // t_new - 1), 0, 0, 0)),
            pl.BlockSpec(head_blk, lambda b, c: (g, q_row_blk0 + b, 0)),
            pl.BlockSpec(head_blk, lambda b, c: (N_GROUPS + g, q_row_blk0 + b, 0)),
            pl.BlockSpec(head_blk, lambda b, c: (2 * N_GROUPS + g, q_row_blk0 + b, 0)),
            pl.BlockSpec(memory_space=pl.ANY),
            pl.BlockSpec((nq, nq), lambda b, c: (0, 0)),
        ],
        out_specs=[
            pl.BlockSpec((None, rc * KV_ROWS, LANES), lambda b, c: (b, c, 0)),
            pl.BlockSpec((t_new, d), lambda b, c: (b, 0)),
            pl.BlockSpec((t_new, d), lambda b, c: (b, 0)),
        ],
        out_shape=[
            jax.ShapeDtypeStruct((nb_, w * KV_ROWS, LANES), F32),
            jax.ShapeDtypeStruct((nb_ * t_new, d), F32),
            jax.ShapeDtypeStruct((nb_ * t_new, d), F32),
        ],
        scratch_shapes=[pltpu.VMEM((nc, nq, ncol), F32),
                        pltpu.SemaphoreType.DMA(()),
                        pltpu.VMEM((nq, LANES), F32),
                        pltpu.VMEM((nq, LANES), F32),
                        pltpu.VMEM((nq, HEAD_DIM_A), F32)],
        compiler_params=_params("arbitrary", "arbitrary"),
        name=f"win_sample_w{w}",
    )(cache, cache, qkv_hm, qkv_hm, qkv_hm, bias_past, bias_new)


def _merge_groups_kernel(o0, o1, o2, l0, l1, l2, out_ref):
    ls = [l0[...], l1[...], l2[...]]
    mx = jnp.maximum(jnp.maximum(ls[0], ls[1]), ls[2])
    es = [jnp.exp(x - mx) for x in ls]
    den = es[0] + es[1] + es[2]
    out_ref[...] = (es[0] / den) * o0[...] + (es[1] / den) * o1[...] + (es[2] / den) * o2[...]


def _merge_groups(outs, lses):
    m, d = outs[0].shape
    tm = _pick(m, (256, 128, 64, 32, 16, 8))
    spec = pl.BlockSpec((tm, d), lambda i: (i, 0))
    return pl.pallas_call(
        _merge_groups_kernel,
        grid=(m // tm,),
        in_specs=[spec] * 6,
        out_specs=spec,
        out_shape=jax.ShapeDtypeStruct((m, d), F32),
        compiler_params=_params("arbitrary"),
        name="merge_groups",
    )(*outs, *lses)


def _rms(x, g):
    return x * lax.rsqrt(jnp.mean(x * x, axis=-1, keepdims=True) + RMS_EPS) * g


def _mla_cq_kernel(x_ref, w_ref, g_ref, o_ref, wb_ref):
    @pl.when(pl.program_id(0) == 0)
    def _():
        wb_ref[...] = w_ref[...].astype(BF16)

    acc = jnp.dot(x_ref[...].astype(BF16), wb_ref[...], preferred_element_type=F32)
    o_ref[...] = _rms(acc, g_ref[...])


def _mla_cq(x, w_dq, q_norm):
    m, k = x.shape
    n = w_dq.shape[1]
    tm = _pick(m, (512, 256, 128, 64, 32, 16, 8))
    return pl.pallas_call(
        _mla_cq_kernel,
        grid=(m // tm,),
        in_specs=[pl.BlockSpec((tm, k), lambda i: (i, 0)),
                  pl.BlockSpec((k, n), lambda i: (0, 0)),
                  pl.BlockSpec((1, n), lambda i: (0, 0))],
        out_specs=pl.BlockSpec((tm, n), lambda i: (i, 0)),
        out_shape=jax.ShapeDtypeStruct((m, n), F32),
        scratch_shapes=[pltpu.VMEM((k, n), BF16)],
        compiler_params=_params("arbitrary"),
        name="mla_cq",
    )(x, w_dq, q_norm.reshape(1, n))


def _mla_rows_kernel(x_ref, w_ref, g_ref, cos_ref, sin_ref, o_ref, wb_ref):
    @pl.when(pl.program_id(0) == 0)
    def _():
        wb_ref[...] = w_ref[...].astype(BF16)

    acc = jnp.dot(x_ref[...].astype(BF16), wb_ref[...], preferred_element_type=F32)
    o_ref[:, :KV_LORA] = _rms(acc[:, :KV_LORA], g_ref[...])
    tail = acc[:, KV_LORA:]
    kr = tail[:, :QK_ROPE] * cos_ref[:, :QK_ROPE] + tail[:, QK_ROPE:] * sin_ref[:, :QK_ROPE]
    o_ref[:, KV_LORA:] = kr


def _mla_rows(x, w_ext, kv_norm, cos_t, sin_t):
    m, k = x.shape
    n = w_ext.shape[1]
    tm = _pick(m, (512, 256, 128, 64, 32, 16, 8))
    return pl.pallas_call(
        _mla_rows_kernel,
        grid=(m // tm,),
        in_specs=[pl.BlockSpec((tm, k), lambda i: (i, 0)),
                  pl.BlockSpec((k, n), lambda i: (0, 0)),
                  pl.BlockSpec((1, KV_LORA), lambda i: (0, 0)),
                  pl.BlockSpec((tm, LANES), lambda i: (i, 0)),
                  pl.BlockSpec((tm, LANES), lambda i: (i, 0))],
        out_specs=pl.BlockSpec((tm, MLA_ROW), lambda i: (i, 0)),
        out_shape=jax.ShapeDtypeStruct((m, MLA_ROW), F32),
        scratch_shapes=[pltpu.VMEM((k, n), BF16)],
        compiler_params=_params("arbitrary"),
        name="mla_rows",
    )(x, w_ext, kv_norm.reshape(1, KV_LORA), cos_t, sin_t)


def _mla_q_kernel(cq_ref, w_ref, wuk_ref, cos_ref, sin_ref, o_ref, wb_ref, wukb_ref):
    @pl.when(pl.program_id(0) == 0)
    def _():
        wb_ref[...] = w_ref[...].astype(BF16)
        wukb_ref[...] = wuk_ref[...].astype(BF16)

    hw = N_HEADS_B * LANES
    q = jnp.dot(cq_ref[...].astype(BF16), wb_ref[...], preferred_element_type=F32)
    cos = cos_ref[:, :QK_ROPE]
    sin = sin_ref[:, :QK_ROPE]
    for h in range(N_HEADS_B):
        nope = q[:, h * LANES:(h + 1) * LANES].astype(BF16)
        wuk_h = wukb_ref[:, h * QK_NOPE:(h + 1) * QK_NOPE]
        o_ref[h, :, :KV_LORA] = lax.dot_general(nope, wuk_h, NT_DIMS, preferred_element_type=F32)
        r0 = hw + h * LANES
        o_ref[h, :, KV_LORA:] = (q[:, r0:r0 + QK_ROPE] * cos
                                 + q[:, hw + r0:hw + r0 + QK_ROPE] * sin)


def _mla_q(cq, w_uq_ext, w_uk2, cos_t, sin_t):
    m, k = cq.shape
    n = w_uq_ext.shape[1]
    tm = _pick(m, (256, 128, 64, 32, 16, 8))
    return pl.pallas_call(
        _mla_q_kernel,
        grid=(m // tm,),
        in_specs=[pl.BlockSpec((tm, k), lambda i: (i, 0)),
                  pl.BlockSpec((k, n), lambda i: (0, 0)),
                  pl.BlockSpec(w_uk2.shape, lambda i: (0, 0)),
                  pl.BlockSpec((tm, LANES), lambda i: (i, 0)),
                  pl.BlockSpec((tm, LANES), lambda i: (i, 0))],
        out_specs=pl.BlockSpec((N_HEADS_B, tm, MLA_ROW), lambda i: (0, i, 0)),
        out_shape=jax.ShapeDtypeStruct((N_HEADS_B, m, MLA_ROW), F32),
        scratch_shapes=[pltpu.VMEM((k, n), BF16), pltpu.VMEM(w_uk2.shape, BF16)],
        compiler_params=_params("arbitrary"),
        name="mla_q",
    )(cq, w_uq_ext, w_uk2, cos_t, sin_t)


MLA_KV_TILE = 512


def _mla_prompt_kernel(q_ref, rows_ref, wuv_ref, o_ref, kb_ref, m_ref, l_ref, acc_ref):
    qi = pl.program_id(1)
    nq = N_HEADS_B * QBLK

    @pl.when(qi == 0)
    def _():
        kb_ref[...] = rows_ref[...].astype(BF16)

    q = q_ref[...].reshape(nq, MLA_ROW).astype(BF16)
    m_ref[...] = jnp.full(m_ref.shape, NEG_INF, F32)
    l_ref[...] = jnp.zeros(l_ref.shape, F32)
    acc_ref[...] = jnp.zeros(acc_ref.shape, F32)
    qpos = qi * QBLK + (lax.broadcasted_iota(jnp.int32, (nq, MLA_KV_TILE), 0) & (QBLK - 1))
    kcol = lax.broadcasted_iota(jnp.int32, (nq, MLA_KV_TILE), 1)

    def kv_step(j, carry):
        k = kb_ref[pl.ds(pl.multiple_of(j * MLA_KV_TILE, MLA_KV_TILE), MLA_KV_TILE), :]
        s = lax.dot_general(q, k, NT_DIMS, preferred_element_type=F32) * MLA_SCALE
        s = jnp.where(kcol + j * MLA_KV_TILE <= qpos, s, NEG_INF)
        m_prev = m_ref[...]
        m_new = jnp.maximum(m_prev, jnp.max(s, axis=-1, keepdims=True))
        alpha = jnp.exp(m_prev - m_new)
        p = jnp.exp(s - m_new[:, :1])
        l_ref[...] = alpha * l_ref[...] + jnp.sum(p, axis=-1, keepdims=True)
        acc_ref[...] = alpha[:, :1] * acc_ref[...] + jnp.dot(p.astype(BF16), k[:, :KV_LORA],
                                                             preferred_element_type=F32)
        m_ref[...] = m_new
        return carry

    lax.fori_loop(0, (qi * QBLK) // MLA_KV_TILE + 1, kv_step, 0)

    for h in range(N_HEADS_B):
        rs = slice(h * QBLK, (h + 1) * QBLK)
        o_lat = (acc_ref[rs, :] / l_ref[rs, :1]).astype(BF16)
        o_ref[:, h * V_HEAD:(h + 1) * V_HEAD] = jnp.dot(
            o_lat, wuv_ref[:, h * V_HEAD:(h + 1) * V_HEAD], preferred_element_type=F32).astype(o_ref.dtype)


def _mla_prompt(qcat, rows, wuv_b, batch, seq):
    nqb = seq // QBLK
    nq = N_HEADS_B * QBLK
    return pl.pallas_call(
        _mla_prompt_kernel,
        grid=(batch, nqb),
        in_specs=[pl.BlockSpec((N_HEADS_B, QBLK, MLA_ROW), lambda b, i: (0, b * nqb + i, 0)),
                  pl.BlockSpec((seq, MLA_ROW), lambda b, i: (b, 0)),
                  pl.BlockSpec(wuv_b.shape, lambda b, i: (0, 0))],
        out_specs=pl.BlockSpec((QBLK, N_HEADS_B * V_HEAD), lambda b, i: (b * nqb + i, 0)),
        out_shape=jax.ShapeDtypeStruct((batch * seq, N_HEADS_B * V_HEAD), F32),
        scratch_shapes=[pltpu.VMEM((seq, MLA_ROW), BF16),
                        pltpu.VMEM((nq, LANES), F32),
                        pltpu.VMEM((nq, LANES), F32),
                        pltpu.VMEM((nq, KV_LORA), F32)],
        compiler_params=_params("arbitrary", "arbitrary"),
        name="mla_prompt",
    )(qcat, rows, wuv_b)


MLA_PAGES_PER_STEP = 16


def _mla_sample_kernel(pt_ref, *refs, t_new, n_chunks):
    del pt_ref
    npg = MLA_PAGES_PER_STEP
    q_ref = refs[0]
    pages = refs[1:1 + npg]
    new_ref, wuv_ref, o_ref, m_ref, l_ref, acc_ref = refs[1 + npg:]
    c = pl.program_id(1)
    nq = N_HEADS_B * t_new

    @pl.when(c == 0)
    def _():
        m_ref[...] = jnp.full(m_ref.shape, NEG_INF, F32)
        l_ref[...] = jnp.zeros(l_ref.shape, F32)
        acc_ref[...] = jnp.zeros(acc_ref.shape, F32)

    q = q_ref[...].reshape(nq, MLA_ROW).astype(BF16)

    def update(scores, pv):
        m_prev = m_ref[...]
        m_new = m_prev
        for s in scores:
            m_new = jnp.maximum(m_new, jnp.max(s, axis=-1, keepdims=True))
        alpha = jnp.exp(m_prev - m_new)
        l_new = alpha * l_ref[...]
        acc = alpha[:, :1] * acc_ref[...]
        for i, s in enumerate(scores):
            p = jnp.exp(s - m_new[:, :1])
            l_new = l_new + jnp.sum(p, axis=-1, keepdims=True)
            acc = acc + pv(i, p.astype(BF16))
        m_ref[...] = m_new
        l_ref[...] = l_new
        acc_ref[...] = acc

    keys_t = [p_ref[...].astype(BF16) for p_ref in pages]
    scores = [jnp.dot(q, kt, preferred_element_type=F32) * MLA_SCALE for kt in keys_t]
    update(scores, lambda i, p: lax.dot_general(p, keys_t[i][:KV_LORA, :], NT_DIMS,
                                                preferred_element_type=F32))

    @pl.when(c == n_chunks - 1)
    def _():
        page = pages[0].shape[1]
        k = jnp.concatenate([new_ref[...], jnp.zeros((page - t_new, MLA_ROW), F32)], axis=0).astype(BF16)
        s = lax.dot_general(q, k, NT_DIMS, preferred_element_type=F32) * MLA_SCALE
        tq = lax.broadcasted_iota(jnp.int32, s.shape, 0) % t_new
        col = lax.broadcasted_iota(jnp.int32, s.shape, 1)
        s = jnp.where(col <= tq, s, NEG_INF)
        update([s], lambda i, p: jnp.dot(p, k[:, :KV_LORA], preferred_element_type=F32))
        for h in range(N_HEADS_B):
            rs = slice(h * t_new, (h + 1) * t_new)
            o_lat = (acc_ref[rs, :] / l_ref[rs, :1]).astype(BF16)
            o_ref[:, h * V_HEAD:(h + 1) * V_HEAD] = jnp.dot(
                o_lat, wuv_ref[:, h * V_HEAD:(h + 1) * V_HEAD], preferred_element_type=F32)


def _mla_sample(qcat, q_row_blk0, rows, pool_t, li, page_table, wuv_b, t_new):
    nb_, n_pages = page_table.shape
    page = pool_t.shape[3]
    npg = MLA_PAGES_PER_STEP
    n_chunks = n_pages // npg
    nq = N_HEADS_B * t_new
    page_specs = [
        pl.BlockSpec((None, None, MLA_ROW, page),
                     functools.partial(lambda b, c, pt, k: (li, pt[b, c * npg + k], 0, 0), k=k))
        for k in range(npg)]
    grid_spec = pltpu.PrefetchScalarGridSpec(
        num_scalar_prefetch=1,
        grid=(nb_, n_chunks),
        in_specs=[pl.BlockSpec((N_HEADS_B, t_new, MLA_ROW), lambda b, c, pt: (0, q_row_blk0 + b, 0))]
        + page_specs
        + [pl.BlockSpec((t_new, MLA_ROW), lambda b, c, pt: (q_row_blk0 + b, 0)),
           pl.BlockSpec(wuv_b.shape, lambda b, c, pt: (0, 0))],
        out_specs=pl.BlockSpec((t_new, N_HEADS_B * V_HEAD), lambda b, c, pt: (b, 0)),
        scratch_shapes=[pltpu.VMEM((nq, LANES), F32),
                        pltpu.VMEM((nq, LANES), F32),
                        pltpu.VMEM((nq, KV_LORA), F32)])
    return pl.pallas_call(
        functools.partial(_mla_sample_kernel, t_new=t_new, n_chunks=n_chunks),
        grid_spec=grid_spec,
        out_shape=jax.ShapeDtypeStruct((nb_ * t_new, N_HEADS_B * V_HEAD), F32),
        compiler_params=_params("arbitrary", "arbitrary"),
        name="mla_sample",
    )(page_table, qcat, *([pool_t] * npg), rows, wuv_b)


ROUTE_TM = 256


def _router_kernel(x_ref, w_ref, b_ref, route_ref, gate_ref, cnt_ref, run_ref):
    @pl.when(pl.program_id(0) == 0)
    def _():
        run_ref[...] = jnp.zeros(run_ref.shape, F32)

    x = x_ref[...]
    w = w_ref[...]
    xh = x.astype(BF16)
    xl = (x - xh.astype(F32)).astype(BF16)
    wh = w.astype(BF16)
    wl = (w - wh.astype(F32)).astype(BF16)
    logits = (jnp.dot(xh, wh, preferred_element_type=F32) + jnp.dot(xh, wl, preferred_element_type=F32)
              + jnp.dot(xl, wh, preferred_element_type=F32)) + b_ref[...]
    n_exp = logits.shape[-1]
    lane = lax.broadcasted_iota(jnp.int32, logits.shape, 1).astype(F32)
    out_lane = lax.broadcasted_iota(jnp.int32, route_ref.shape, 1)
    route = jnp.zeros(route_ref.shape, F32)
    val_out = jnp.zeros(gate_ref.shape, F32)
    sel = jnp.zeros(logits.shape, F32)
    top0 = None
    denom = None
    vals, idxs = [], []
    for k in range(TOP_K):
        mx = jnp.max(logits, axis=-1, keepdims=True)
        idx = jnp.min(jnp.where(logits == mx, lane, float(n_exp)), axis=-1, keepdims=True)
        if k == 0:
            top0 = mx
        e = jnp.exp(mx - top0)
        vals.append(e)
        idxs.append(idx)
        denom = e if denom is None else denom + e
        route = jnp.where(out_lane == k, idx, route)
        sel = sel + jnp.where(lane == idx, 1.0, 0.0)
        logits = jnp.where(lane == idx, -jnp.inf, logits)
    for k in range(TOP_K):
        val_out = jnp.where(out_lane == k, vals[k] / denom, val_out)
    tm = x.shape[0]
    before = lax.broadcasted_iota(jnp.int32, (tm, tm), 1) < lax.broadcasted_iota(jnp.int32, (tm, tm), 0)
    rank = jnp.dot(jnp.where(before, 1.0, 0.0).astype(BF16), sel.astype(BF16),
                   preferred_element_type=F32) + run_ref[...]
    for k in range(TOP_K):
        rank_k = jnp.sum(jnp.where(lane == idxs[k], rank, 0.0), axis=-1, keepdims=True)
        route = jnp.where(out_lane == TOP_K + k, rank_k, route)
    run_ref[...] = run_ref[...] + jnp.sum(sel, axis=0, keepdims=True)
    cnt_ref[...] = run_ref[...]
    route_ref[...] = route.astype(jnp.int32)
    gate_ref[...] = val_out


def _router(x, w, b):
    m, d = x.shape
    n_exp = w.shape[1]
    tm = _pick(m, (ROUTE_TM, 128, 64, 32, 16, 8))
    return pl.pallas_call(
        _router_kernel,
        grid=(m // tm,),
        in_specs=[pl.BlockSpec((tm, d), lambda i: (i, 0)),
                  pl.BlockSpec((d, n_exp), lambda i: (0, 0)),
                  pl.BlockSpec((1, n_exp), lambda i: (0, 0))],
        out_specs=[pl.BlockSpec((tm, LANES), lambda i: (i, 0)),
                   pl.BlockSpec((tm, LANES), lambda i: (i, 0)),
                   pl.BlockSpec((1, n_exp), lambda i: (0, 0))],
        out_shape=[jax.ShapeDtypeStruct((m, LANES), jnp.int32),
                   jax.ShapeDtypeStruct((m, LANES), F32),
                   jax.ShapeDtypeStruct((1, n_exp), F32)],
        scratch_shapes=[pltpu.VMEM((1, n_exp), F32)],
        compiler_params=_params("arbitrary"),
        name="router",
    )(x, w, b.reshape(1, n_exp))


def _row_copies(n_rows, make_copy):
    def issue(r, carry):
        for k in range(TOP_K):
            make_copy(r, k).start()
        return carry

    def drain(r, carry):
        for k in range(TOP_K):
            make_copy(r, k).wait()
        return carry

    lax.fori_loop(0, n_rows, issue, 0)
    lax.fori_loop(0, n_rows, drain, 0)


def _dispatch_kernel(dest_ref, x_ref, xs_in, xs_ref, sem, *, tm):
    del xs_in
    base = pl.program_id(0) * (tm * TOP_K)
    _row_copies(tm, lambda r, k: pltpu.make_async_copy(
        x_ref.at[pl.ds(r, 1)], xs_ref.at[pl.ds(dest_ref[base + r * TOP_K + k], 1)], sem))


def _dispatch(dest_flat, x, n_rows):
    n_tok, d = x.shape
    tm = _pick(n_tok, (ROUTE_TM, 128, 64, 32, 16, 8))
    grid_spec = pltpu.PrefetchScalarGridSpec(
        num_scalar_prefetch=1,
        grid=(n_tok // tm,),
        in_specs=[pl.BlockSpec((tm, d), lambda i, dest: (i, 0)),
                  pl.BlockSpec(memory_space=pl.ANY)],
        out_specs=pl.BlockSpec(memory_space=pl.ANY),
        scratch_shapes=[pltpu.SemaphoreType.DMA(())])
    return pl.pallas_call(
        functools.partial(_dispatch_kernel, tm=tm),
        grid_spec=grid_spec,
        out_shape=jax.ShapeDtypeStruct((n_rows, d), x.dtype),
        input_output_aliases={2: 0},
        compiler_params=_params("arbitrary"),
        name="moe_dispatch",
    )(dest_flat, x, jnp.zeros((n_rows, d), x.dtype))


MOE_TM = 512
MOE_TF = 512
MOE_TN = 1024


def _tile_changed(meta_ref, t):
    return jnp.logical_or(t == 0, meta_ref[1 + t] != meta_ref[jnp.maximum(t, 1)])


def _moe_up_kernel(meta_ref, x_ref, wg_ref, wu_ref, bg_ref, bu_ref, h_ref, wgb_ref, wub_ref):
    t = pl.program_id(1)
    used = t < meta_ref[0]

    @pl.when(jnp.logical_and(used, _tile_changed(meta_ref, t)))
    def _():
        wgb_ref[...] = wg_ref[...].astype(BF16)
        wub_ref[...] = wu_ref[...].astype(BF16)

    @pl.when(used)
    def _():
        x = x_ref[...].astype(BF16)
        g = jnp.minimum(jnp.dot(x, wgb_ref[...], preferred_element_type=F32) + bg_ref[...], SWIGLU_LIMIT)
        u = jnp.clip(jnp.dot(x, wub_ref[...], preferred_element_type=F32) + bu_ref[...],
                     -SWIGLU_LIMIT, SWIGLU_LIMIT)
        h_ref[...] = ((u + 1.0) * (g * (1.0 / (1.0 + jnp.exp(-SWIGLU_ALPHA * g))))).astype(h_ref.dtype)

    @pl.when(jnp.logical_not(used))
    def _():
        h_ref[...] = jnp.zeros(h_ref.shape, h_ref.dtype)


def _moe_up(meta, xs, w_gate, w_up, b_gate, b_up, layer):
    n_rows, d = xs.shape
    n_exp, dff = b_gate.shape[1:]
    n_tiles = n_rows // MOE_TM
    tf = _pick(dff, (MOE_TF, 256, 128))
    wspec = pl.BlockSpec((None, None, d, tf), lambda f, t, mt: (layer, mt[1 + t], 0, f))
    bspec = pl.BlockSpec((None, None, 1, tf), lambda f, t, mt: (layer, mt[1 + t], 0, f))
    grid_spec = pltpu.PrefetchScalarGridSpec(
        num_scalar_prefetch=1,
        grid=(dff // tf, n_tiles),
        in_specs=[pl.BlockSpec((MOE_TM, d), lambda f, t, mt: (t, 0)), wspec, wspec, bspec, bspec],
        out_specs=pl.BlockSpec((MOE_TM, tf), lambda f, t, mt: (t, f)),
        scratch_shapes=[pltpu.VMEM((d, tf), BF16), pltpu.VMEM((d, tf), BF16)])
    return pl.pallas_call(
        _moe_up_kernel,
        grid_spec=grid_spec,
        out_shape=jax.ShapeDtypeStruct((n_rows, dff), BF16),
        compiler_params=_params("arbitrary", "arbitrary"),
        name="moe_up",
    )(meta, xs, w_gate, w_up, b_gate.reshape(-1, n_exp, 1, dff), b_up.reshape(-1, n_exp, 1, dff))


def _moe_down_kernel(meta_ref, h_ref, wd_ref, bd_ref, y_ref, wdb_ref):
    t = pl.program_id(1)
    used = t < meta_ref[0]

    @pl.when(jnp.logical_and(used, _tile_changed(meta_ref, t)))
    def _():
        wdb_ref[...] = wd_ref[...].astype(BF16)

    @pl.when(used)
    def _():
        y_ref[...] = jnp.dot(h_ref[...], wdb_ref[...], preferred_element_type=F32) + bd_ref[...]

    @pl.when(jnp.logical_not(used))
    def _():
        y_ref[...] = jnp.zeros(y_ref.shape, y_ref.dtype)


def _moe_down(meta, hid, w_down, b_down, layer):
    n_rows, dff = hid.shape
    n_exp, d = b_down.shape[1:]
    n_tiles = n_rows // MOE_TM
    tn = _pick(d, (MOE_TN, 512, 256, 128))
    grid_spec = pltpu.PrefetchScalarGridSpec(
        num_scalar_prefetch=1,
        grid=(d // tn, n_tiles),
        in_specs=[pl.BlockSpec((MOE_TM, dff), lambda n, t, mt: (t, 0)),
                  pl.BlockSpec((None, None, dff, tn), lambda n, t, mt: (layer, mt[1 + t], 0, n)),
                  pl.BlockSpec((None, None, 1, tn), lambda n, t, mt: (layer, mt[1 + t], 0, n))],
        out_specs=pl.BlockSpec((MOE_TM, tn), lambda n, t, mt: (t, n)),
        scratch_shapes=[pltpu.VMEM((dff, tn), BF16)])
    return pl.pallas_call(
        _moe_down_kernel,
        grid_spec=grid_spec,
        out_shape=jax.ShapeDtypeStruct((n_rows, d), F32),
        compiler_params=_params("arbitrary", "arbitrary"),
        name="moe_down",
    )(meta, hid, w_down, b_down.reshape(-1, n_exp, 1, d))


def _combine_ln_kernel(dest_ref, x_ref, gate_ref, g_ref, b_ref, yr_ref, o_ref, buf_ref, sem, *, alpha, tm):
    base = pl.program_id(0) * (tm * TOP_K)
    _row_copies(tm, lambda r, k: pltpu.make_async_copy(
        yr_ref.at[pl.ds(dest_ref[base + r * TOP_K + k], 1)], buf_ref.at[k, pl.ds(r, 1)], sem))
    gate = gate_ref[...]
    y = gate[:, 0:1] * buf_ref[0]
    for k in range(1, TOP_K):
        y = y + gate[:, k:k + 1] * buf_ref[k]
    z = alpha * x_ref[...] + y
    mu = jnp.mean(z, axis=-1, keepdims=True)
    zc = z - mu
    var = jnp.mean(zc * zc, axis=-1, keepdims=True)
    o_ref[...] = zc * lax.rsqrt(var + LN_EPS) * g_ref[...] + b_ref[...]


def _combine_ln(dest_flat, x, gate_pad, yr, g, b, alpha):
    n_tok, d = x.shape
    tm = _pick(n_tok, (ROUTE_TM, 128, 64, 32, 16, 8))
    grid_spec = pltpu.PrefetchScalarGridSpec(
        num_scalar_prefetch=1,
        grid=(n_tok // tm,),
        in_specs=[pl.BlockSpec((tm, d), lambda i, dest: (i, 0)),
                  pl.BlockSpec((tm, LANES), lambda i, dest: (i, 0)),
                  pl.BlockSpec((1, d), lambda i, dest: (0, 0)),
                  pl.BlockSpec((1, d), lambda i, dest: (0, 0)),
                  pl.BlockSpec(memory_space=pl.ANY)],
        out_specs=pl.BlockSpec((tm, d), lambda i, dest: (i, 0)),
        scratch_shapes=[pltpu.VMEM((TOP_K, tm, d), F32), pltpu.SemaphoreType.DMA(())])
    return pl.pallas_call(
        functools.partial(_combine_ln_kernel, alpha=alpha, tm=tm),
        grid_spec=grid_spec,
        out_shape=jax.ShapeDtypeStruct((n_tok, d), F32),
        compiler_params=_params("arbitrary"),
        name="moe_combine_ln",
    )(dest_flat, x, gate_pad, g.reshape(1, d), b.reshape(1, d), yr)


def _moe_ln(x, layer, router_w, router_b, w_gate, b_gate, w_up, b_up, w_down, b_down, g, b, alpha):
    n_tok, d = x.shape
    n_exp = router_w.shape[-1]
    route, gate_pad, cnt = _router(x, router_w[layer], router_b[layer])
    counts = cnt[0].astype(jnp.int32)
    padded = (counts + MOE_TM - 1) // MOE_TM * MOE_TM
    pend = jnp.cumsum(padded)
    pstart = pend - padded
    n_tiles = -(-(n_tok * TOP_K + n_exp * (MOE_TM - 1)) // MOE_TM)
    n_rows = n_tiles * MOE_TM
    tile_e = jnp.minimum(jnp.sum(pend[None, :] <= (jnp.arange(n_tiles, dtype=jnp.int32) * MOE_TM)[:, None], axis=1),
                         n_exp - 1)
    meta = jnp.concatenate([pend[-1:] // MOE_TM, tile_e]).astype(jnp.int32)
    top_e = route[:, :TOP_K]
    start = jnp.sum(jnp.where(top_e[:, :, None] == jnp.arange(n_exp, dtype=jnp.int32), pstart, 0), axis=-1)
    dest_flat = (start + route[:, TOP_K:2 * TOP_K]).reshape(-1)
    xs = _dispatch(dest_flat, x, n_rows)
    hid = _moe_up(meta, xs, w_gate, w_up, b_gate, b_up, layer)
    yr = _moe_down(meta, hid, w_down, b_down, layer)
    return _combine_ln(dest_flat, x, gate_pad, yr, g, b, alpha)


def _rope_tables(pos):
    half = QK_ROPE // 2
    inv = ROPE_BASE ** (-jnp.arange(half, dtype=F32) / half)
    ang = pos.astype(F32)[:, None] * inv[None]
    cos, sin = jnp.cos(ang), jnp.sin(ang)
    pad = jnp.zeros((pos.shape[0], LANES - QK_ROPE), F32)
    return (jnp.concatenate([cos, cos, pad], axis=1), jnp.concatenate([sin, sin, pad], axis=1))


def _rot_cols(w):
    half = w.shape[-1] // 2
    return jnp.concatenate([-w[..., half:], w[..., :half]], axis=-1)


def kernel(x_prompt, x_sample, cache_win_w128, cache_win_w512, cache_win_w2048, cache_mla, page_table, rel_bias, w_qkv_a, w_o_a, w_dq, q_norm, w_uq, w_dkv, kv_norm, w_uk, w_uv, w_o_b, ln1_g, ln1_b, ln2_g, ln2_b, router_w, router_b, w_gate, b_gate, w_up, b_up, w_down, b_down):
    batch, seq, d_model = x_prompt.shape
    dec_batch, t_new, _ = x_sample.shape
    depth = ln1_g.shape[0]
    alpha = (2 * depth) ** 0.25
    n_p = batch * seq
    n_s = dec_batch * t_new
    past_len = page_table.shape[1] * cache_mla.shape[2]
    assert t_new == SUBLANES and seq % (QBLK * DIL_GROUPS[-1][1]) == 0 and n_p % t_new == 0
    s_blk0 = n_p // t_new
    win_caches = (cache_win_w128, cache_win_w512, cache_win_w2048)
    hd = N_SLOTS * HEAD_DIM_A

    x = jnp.concatenate([x_prompt.reshape(n_p, d_model), x_sample.reshape(n_s, d_model)], axis=0)
    win_p = [[] for _ in DIL_GROUPS]
    win_s = [[] for _ in DIL_GROUPS]
    mla_p, mla_s = [], []

    for layer in range(depth):
        li = layer // 2
        if layer % 2 == 0:
            qkv_hm = _qkv_proj(x.astype(BF16), w_qkv_a[li])
            bias_mats = []
            qi = jnp.arange(QBLK)[:, None]
            ki = jnp.arange(2 * QBLK)[None, :]
            for g, (window, dil) in enumerate(DIL_GROUPS):
                n_taps = window // dil + 1
                bias_j = _tap_bias(rel_bias, g, dil, n_taps)
                bias_mats.append(_tap_table(bias_j, QBLK + qi - ki, n_taps))
            o_p = _dil_prompt(qkv_hm, bias_mats, batch, seq)
            for g, (window, dil) in enumerate(DIL_GROUPS):
                keep = min(window, seq)
                st = _win_states(qkv_hm, g, batch, seq, keep)
                win_p[g].append(st.reshape(batch, keep, 2, N_SLOTS, HEAD_DIM_A))
            outs, lses = [], []
            same_slot = jnp.eye(N_SLOTS, dtype=bool)
            for g, (window, dil) in enumerate(DIL_GROUPS):
                n_taps = window // dil + 1
                cache = win_caches[g]
                lb = cache.shape[2]
                assert lb == window and lb % t_new == 0
                bias_j = _tap_bias(rel_bias, g, dil, n_taps)
                i_q = jnp.arange(t_new)[:, None]

                def table(dist):
                    tap = jnp.where(dist % dil == 0, dist // dil, -1)
                    return _tap_table(bias_j, tap, n_taps)

                rc = min(lb, 256)
                t_past = table(lb + i_q - jnp.arange(lb)[None, :])
                b_past = jnp.where(same_slot[:, None, None, :], t_past[..., None], NEG_INF)
                b_past = b_past.reshape(N_SLOTS * t_new, lb // rc, rc * N_SLOTS).transpose(1, 0, 2)
                t_nw = table(i_q - jnp.arange(t_new)[None, :])
                b_new = jnp.where(same_slot[:, None, :, None], t_nw[:, :, None, :], NEG_INF)
                b_new = b_new.reshape(N_SLOTS * t_new, N_SLOTS * t_new)
                win_out, o_g, lse_g = _win_sample(cache, li, qkv_hm, s_blk0, g, b_past, b_new, t_new)
                win_s[g].append(win_out.reshape(dec_batch, lb, 2, N_SLOTS, HEAD_DIM_A))
                outs.append(o_g)
                lses.append(lse_g)
            o_s = _merge_groups(outs, lses)
            o_all = jnp.concatenate([o_p, o_s.astype(BF16)], axis=0)
            mix = _matmul(o_all, w_o_a[li])
        else:
            pos = jnp.concatenate([jnp.tile(jnp.arange(seq, dtype=jnp.int32), batch),
                                   jnp.tile(past_len + jnp.arange(t_new, dtype=jnp.int32), dec_batch)])
            cos_t, sin_t = _rope_tables(pos)
            w_kv = w_dkv[li]
            w_kv_ext = jnp.concatenate([w_kv, _rot_cols(w_kv[:, KV_LORA:])], axis=1)
            rows = _mla_rows(x, w_kv_ext, kv_norm[li], cos_t, sin_t)
            mla_p.append(rows[:n_p].reshape(batch, seq, MLA_ROW))
            mla_s.append(rows[n_p:].reshape(dec_batch, t_new, MLA_ROW))
            cq = _mla_cq(x, w_dq[li], q_norm[li])
            wq = w_uq[li].reshape(-1, N_HEADS_B, QK_NOPE + QK_ROPE)
            zpad = jnp.zeros(wq.shape[:2] + (LANES - QK_ROPE,), wq.dtype)
            w_rope = wq[..., QK_NOPE:]
            w_uq_ext = jnp.concatenate([
                wq[..., :QK_NOPE].reshape(wq.shape[0], -1),
                jnp.concatenate([w_rope, zpad], axis=-1).reshape(wq.shape[0], -1),
                jnp.concatenate([_rot_cols(w_rope), zpad], axis=-1).reshape(wq.shape[0], -1)], axis=1)
            qcat = _mla_q(cq, w_uq_ext, w_uk[li].reshape(KV_LORA, N_HEADS_B * QK_NOPE), cos_t, sin_t)
            wuv_b = w_uv[li].reshape(KV_LORA, N_HEADS_B * V_HEAD).astype(BF16)
            v_p = _mla_prompt(qcat, rows, wuv_b, batch, seq)
            v_s = _mla_sample(qcat, s_blk0, rows, jnp.swapaxes(cache_mla, 2, 3), li, page_table, wuv_b, t_new)
            mix = _matmul(jnp.concatenate([v_p, v_s], axis=0), w_o_b[li])
        x = _add_ln(x, mix, ln1_g[layer], ln1_b[layer], alpha)
        x = _moe_ln(x, layer, router_w, router_b, w_gate, b_gate, w_up, b_up, w_down, b_down,
                    ln2_g[layer], ln2_b[layer], alpha)

    xp = x[:n_p].reshape(batch, seq, d_model)
    xs = x[n_p:].reshape(dec_batch, t_new, d_model)
    return (xp, xs,
            jnp.stack(win_p[0], 0), jnp.stack(win_p[1], 0), jnp.stack(win_p[2], 0), jnp.stack(mla_p, 0),
            jnp.stack(win_s[0], 0), jnp.stack(win_s[1], 0), jnp.stack(win_s[2], 0), jnp.stack(mla_s, 0))
```

```python
import functools
import math

import jax
import jax.numpy as jnp
from jax import lax
from jax.experimental import pallas as pl
from jax.experimental.pallas import tpu as pltpu

F32 = jnp.float32
BF16 = jnp.bfloat16

DIL_GROUPS = ((128, 1), (512, 4), (2048, 16))
N_GROUPS = len(DIL_GROUPS)
N_SLOTS = 16
HEAD_DIM_A = 128
QBLK = 128
A_SCALE = HEAD_DIM_A ** -0.5
N_BUCKETS = 32
MAX_DISTANCE = 2048
N_HEADS_B = 16
KV_LORA = 512
QK_NOPE = 128
QK_ROPE = 64
V_HEAD = 128
ROPE_BASE = 10000.0
MLA_SCALE = (QK_NOPE + QK_ROPE) ** -0.5
MLA_ROW = KV_LORA + QK_ROPE
TOP_K = 4
SWIGLU_ALPHA = 1.702
SWIGLU_LIMIT = 7.0
LN_EPS = 1e-5
RMS_EPS = 1e-6
NEG_INF = -1e30

LANES = 128
SUBLANES = 8
VMEM_LIMIT = 56 * 1024 * 1024
KV_ROWS = 2 * N_SLOTS
QKV_TN = 1024

NT_DIMS = (((1,), (1,)), ((), ()))


def _pick(n, candidates):
    for c in candidates:
        if n % c == 0:
            return c
    raise ValueError(f"no tile in {candidates} divides {n}")


def _params(*sem):
    return pltpu.CompilerParams(dimension_semantics=sem, vmem_limit_bytes=VMEM_LIMIT)


def _mm_kernel(x_ref, w_ref, o_ref, wb_ref):
    @pl.when(pl.program_id(1) == 0)
    def _():
        wb_ref[...] = w_ref[...].astype(BF16)

    o_ref[...] = jnp.dot(x_ref[...].astype(BF16), wb_ref[...],
                         preferred_element_type=F32).astype(o_ref.dtype)


def _matmul(x, w, out_dtype=F32):
    m, k = x.shape
    n = w.shape[1]
    tm = _pick(m, (512, 256, 128, 64, 32, 16, 8))
    tn = _pick(n, (512, 256, 128))
    return pl.pallas_call(
        _mm_kernel,
        grid=(n // tn, m // tm),
        in_specs=[pl.BlockSpec((tm, k), lambda j, i: (i, 0)),
                  pl.BlockSpec((k, tn), lambda j, i: (0, j))],
        out_specs=pl.BlockSpec((tm, tn), lambda j, i: (i, j)),
        out_shape=jax.ShapeDtypeStruct((m, n), out_dtype),
        scratch_shapes=[pltpu.VMEM((k, tn), BF16)],
        compiler_params=_params("arbitrary", "arbitrary"),
        name="matmul",
    )(x, w)


def _qkv_kernel(x_ref, w_ref, hm_ref, wb_ref, *, heads_per_tile):
    @pl.when(pl.program_id(1) == 0)
    def _():
        wb_ref[...] = w_ref[...].astype(BF16)

    acc = jnp.dot(x_ref[...].astype(BF16), wb_ref[...], preferred_element_type=F32)
    for j in range(heads_per_tile):
        hm_ref[j] = acc[:, j * HEAD_DIM_A:(j + 1) * HEAD_DIM_A]


def _qkv_proj(x, w):
    m, k = x.shape
    n = w.shape[1]
    tm = _pick(m, (512, 256, 128, 64, 32, 16, 8))
    tn = QKV_TN
    hpt = tn // HEAD_DIM_A
    return pl.pallas_call(
        functools.partial(_qkv_kernel, heads_per_tile=hpt),
        grid=(n // tn, m // tm),
        in_specs=[pl.BlockSpec((tm, k), lambda j, i: (i, 0)),
                  pl.BlockSpec((k, tn), lambda j, i: (0, j))],
        out_specs=pl.BlockSpec((hpt, tm, HEAD_DIM_A), lambda j, i: (j, i, 0)),
        out_shape=jax.ShapeDtypeStruct((n // HEAD_DIM_A, m, HEAD_DIM_A), F32),
        scratch_shapes=[pltpu.VMEM((k, tn), BF16)],
        compiler_params=_params("arbitrary", "arbitrary"),
        name="qkv_proj",
    )(x, w)


def _win_states_kernel(k_ref, v_ref, o_ref, *, rows):
    for h in range(N_SLOTS):
        o_ref[pl.ds(h, rows, stride=KV_ROWS), :] = k_ref[h]
        o_ref[pl.ds(N_SLOTS + h, rows, stride=KV_ROWS), :] = v_ref[h]


def _win_states(qkv_hm, g, batch, seq, keep):
    rows = min(keep, 256)
    nck = keep // rows
    first = (seq - keep) // rows
    per_seq = seq // rows
    kblk = N_GROUPS + g
    vblk = 2 * N_GROUPS + g
    return pl.pallas_call(
        functools.partial(_win_states_kernel, rows=rows),
        grid=(batch, nck),
        in_specs=[pl.BlockSpec((N_SLOTS, rows, HEAD_DIM_A), lambda b, c: (kblk, b * per_seq + first + c, 0)),
                  pl.BlockSpec((N_SLOTS, rows, HEAD_DIM_A), lambda b, c: (vblk, b * per_seq + first + c, 0))],
        out_specs=pl.BlockSpec((None, rows * KV_ROWS, LANES), lambda b, c: (b, c, 0)),
        out_shape=jax.ShapeDtypeStruct((batch, keep * KV_ROWS, LANES), F32),
        compiler_params=_params("arbitrary", "arbitrary"),
        name=f"win_states_{keep}",
    )(qkv_hm, qkv_hm)


def _add_ln_kernel(x_ref, y_ref, g_ref, b_ref, o_ref, *, alpha):
    z = alpha * x_ref[...] + y_ref[...].astype(F32)
    mu = jnp.mean(z, axis=-1, keepdims=True)
    zc = z - mu
    var = jnp.mean(zc * zc, axis=-1, keepdims=True)
    o_ref[...] = zc * lax.rsqrt(var + LN_EPS) * g_ref[...] + b_ref[...]


def _add_ln(x, y, g, b, alpha):
    m, d = x.shape
    tm = _pick(m, (512, 256, 128, 64, 32, 16, 8))
    return pl.pallas_call(
        functools.partial(_add_ln_kernel, alpha=alpha),
        grid=(m // tm,),
        in_specs=[pl.BlockSpec((tm, d), lambda i: (i, 0)),
                  pl.BlockSpec((tm, d), lambda i: (i, 0)),
                  pl.BlockSpec((1, d), lambda i: (0, 0)),
                  pl.BlockSpec((1, d), lambda i: (0, 0))],
        out_specs=pl.BlockSpec((tm, d), lambda i: (i, 0)),
        out_shape=jax.ShapeDtypeStruct((m, d), F32),
        compiler_params=_params("arbitrary"),
        name="add_ln",
    )(x, y, g.reshape(1, d), b.reshape(1, d))


def _t5_bucket(dist):
    max_exact = N_BUCKETS // 2
    d = jnp.maximum(dist, 1).astype(F32)
    large = max_exact + (jnp.log(d / max_exact) / math.log(MAX_DISTANCE / max_exact)
                         * (N_BUCKETS - max_exact)).astype(jnp.int32)
    large = jnp.minimum(large, N_BUCKETS - 1)
    return jnp.where(dist < max_exact, dist, large)


def _tap_bias(rel_bias, g, dil, n_taps):
    b = rel_bias[_t5_bucket(jnp.arange(n_taps, dtype=jnp.int32) * dil)]
    return b[:, g * N_SLOTS:(g + 1) * N_SLOTS].T.astype(F32)


def _tap_table(bias_j, tap, n_taps):
    ok = (tap >= 0) & (tap < n_taps)
    onehot = (tap.reshape(1, -1) == jnp.arange(n_taps, dtype=tap.dtype)[:, None]).astype(F32)
    t = jnp.dot(bias_j, onehot, precision=lax.Precision.HIGHEST).reshape(bias_j.shape[0], *tap.shape)
    return jnp.where(ok[None], t, NEG_INF)


def _dil_prompt_kernel(*refs, seq):
    qkv = refs[:9]
    bias = refs[9:12]
    o_ref = refs[12]
    o_scr, lse_scr = refs[13], refs[14]

    def blk(t, g, dil):
        q_ref, k_ref, v_ref = qkv[3 * g], qkv[3 * g + 1], qkv[3 * g + 2]
        nb = seq // dil // QBLK
        r = t // nb
        n = t % nb
        start = r + n * (QBLK * dil)
        pstart = r + jnp.maximum(n - 1, 0) * (QBLK * dil)
        cur = pl.ds(start, QBLK, stride=dil) if dil > 1 else pl.ds(start, QBLK)
        prv = pl.ds(pstart, QBLK, stride=dil) if dil > 1 else pl.ds(pstart, QBLK)
        q = q_ref[cur, :].astype(BF16)
        kk = jnp.concatenate([k_ref[prv, :], k_ref[cur, :]], axis=0).astype(BF16)
        vv = jnp.concatenate([v_ref[prv, :], v_ref[cur, :]], axis=0).astype(BF16)
        s = lax.dot_general(q, kk, NT_DIMS, preferred_element_type=F32) * A_SCALE + bias[g][...]
        ki = lax.broadcasted_iota(jnp.int32, s.shape, 1)
        s = jnp.where((ki >= QBLK) | (n > 0), s, NEG_INF)
        m = jnp.max(s, axis=-1, keepdims=True)
        p = jnp.exp(s - m)
        l = jnp.sum(p, axis=-1, keepdims=True)
        o = jnp.dot(p.astype(BF16), vv, preferred_element_type=F32) / l
        o_scr[g, cur, :] = o
        lse_scr[g, cur, :] = jnp.broadcast_to(m + jnp.log(l), (QBLK, HEAD_DIM_A))

    def blocks(t, carry):
        for g, (_, dil) in enumerate(DIL_GROUPS):
            blk(t, g, dil)
        return carry

    lax.fori_loop(0, seq // QBLK, blocks, 0)

    rows = 256

    def merge(c, carry):
        sl = pl.ds(pl.multiple_of(c * rows, rows), rows)
        ls = [lse_scr[g, sl, :] for g in range(N_GROUPS)]
        mx = jnp.maximum(jnp.maximum(ls[0], ls[1]), ls[2])
        es = [jnp.exp(x - mx) for x in ls]
        den = es[0] + es[1] + es[2]
        out = (es[0] / den) * o_scr[0, sl, :] + (es[1] / den) * o_scr[1, sl, :] + (es[2] / den) * o_scr[2, sl, :]
        o_ref[sl, :] = out.astype(o_ref.dtype)
        return carry

    lax.fori_loop(0, seq // rows, merge, 0)


def _dil_prompt(qkv_hm, bias_mats, batch, seq):
    gh = N_GROUPS * N_SLOTS
    in_specs = []
    for g in range(N_GROUPS):
        for which in range(3):
            in_specs.append(pl.BlockSpec(
                (None, seq, HEAD_DIM_A),
                functools.partial(lambda b, h, base: (base + h, b, 0), base=which * gh + g * N_SLOTS)))
    for g in range(N_GROUPS):
        in_specs.append(pl.BlockSpec((None, QBLK, 2 * QBLK), lambda b, h: (h, 0, 0)))
    return pl.pallas_call(
        functools.partial(_dil_prompt_kernel, seq=seq),
        grid=(batch, N_SLOTS),
        in_specs=in_specs,
        out_specs=pl.BlockSpec((seq, HEAD_DIM_A), lambda b, h: (b, h)),
        out_shape=jax.ShapeDtypeStruct((batch * seq, N_SLOTS * HEAD_DIM_A), BF16),
        scratch_shapes=[pltpu.VMEM((N_GROUPS, seq, HEAD_DIM_A), F32),
                        pltpu.VMEM((N_GROUPS, seq, HEAD_DIM_A), F32)],
        compiler_params=_params("arbitrary", "arbitrary"),
        name="dil_prompt",
    )(*([qkv_hm] * 9), *bias_mats)


def _win_sample_kernel(cache_ref, next_ref, q_ref, kn_ref, vn_ref, bt_hbm, btn_ref,
                       win_ref, o_ref, lse_ref, bt_ref, sem, m_ref, l_ref, acc_ref, *, rc, nc, t_new):
    b = pl.program_id(0)
    c = pl.program_id(1)
    keep = (rc - t_new) * KV_ROWS
    nq = N_SLOTS * t_new

    @pl.when((b == 0) & (c == 0))
    def _():
        cp = pltpu.make_async_copy(bt_hbm, bt_ref, sem)
        cp.start()
        cp.wait()

    @pl.when(c == 0)
    def _():
        m_ref[...] = jnp.full(m_ref.shape, NEG_INF, F32)
        l_ref[...] = jnp.zeros(l_ref.shape, F32)
        acc_ref[...] = jnp.zeros(acc_ref.shape, F32)

    win_ref[0:keep, :] = cache_ref[t_new:].reshape(keep, LANES)

    @pl.when(c < nc - 1)
    def _():
        win_ref[keep:, :] = next_ref[...].reshape(t_new * KV_ROWS, LANES)

    q = q_ref[...].reshape(nq, HEAD_DIM_A).astype(BF16)

    def step(k, v, bias):
        s = lax.dot_general(q, k, NT_DIMS, preferred_element_type=F32) * A_SCALE + bias
        m_prev = m_ref[...]
        m_new = jnp.maximum(m_prev, jnp.max(s, axis=-1, keepdims=True))
        alpha = jnp.exp(m_prev - m_new)
        p = jnp.exp(s - m_new[:, :1])
        l_ref[...] = alpha * l_ref[...] + jnp.sum(p, axis=-1, keepdims=True)
        acc_ref[...] = alpha * acc_ref[...] + jnp.dot(p.astype(BF16), v, preferred_element_type=F32)
        m_ref[...] = m_new

    step(cache_ref[:, 0].reshape(rc * N_SLOTS, HEAD_DIM_A).astype(BF16),
         cache_ref[:, 1].reshape(rc * N_SLOTS, HEAD_DIM_A).astype(BF16), bt_ref[c])

    @pl.when(c == nc - 1)
    def _():
        for h in range(N_SLOTS):
            win_ref[pl.ds(keep + h, t_new, stride=KV_ROWS), :] = kn_ref[h]
            win_ref[pl.ds(keep + N_SLOTS + h, t_new, stride=KV_ROWS), :] = vn_ref[h]
        step(kn_ref[...].reshape(nq, HEAD_DIM_A).astype(BF16),
             vn_ref[...].reshape(nq, HEAD_DIM_A).astype(BF16), btn_ref[...])
        o = acc_ref[...] / l_ref[...]
        lse = m_ref[...] + jnp.log(l_ref[...])
        for h in range(N_SLOTS):
            sl = slice(h * HEAD_DIM_A, (h + 1) * HEAD_DIM_A)
            o_ref[:, sl] = o[h * t_new:(h + 1) * t_new]
            lse_ref[:, sl] = lse[h * t_new:(h + 1) * t_new]


def _win_sample(cache, li, qkv_hm, q_row_blk0, g, bias_past, bias_new, t_new):
    nb_, w = cache.shape[1], cache.shape[2]
    nc, nq, ncol = bias_past.shape
    rc = w // nc
    d = N_SLOTS * HEAD_DIM_A
    kern = functools.partial(_win_sample_kernel, rc=rc, nc=nc, t_new=t_new)
    head_blk = (N_SLOTS, t_new, HEAD_DIM_A)
    return pl.pallas_call(
        kern,
        grid=(nb_, nc),
        in_specs=[
            pl.BlockSpec((None, None, rc, 2, N_SLOTS, HEAD_DIM_A), lambda b, c: (li, b, c, 0, 0, 0)),
            pl.BlockSpec((None, None, t_new, 2, N_SLOTS, HEAD_DIM_A),
                         lambda b, c: (li, b, jnp.minimum((c + 1) * (rc // t_new), w // t_new - 1), 0, 0, 0)),
            pl.BlockSpec(head_blk, lambda b, c: (g, q_row_blk0 + b, 0)),
            pl.BlockSpec(head_blk, lambda b, c: (N_GROUPS + g, q_row_blk0 + b, 0)),
            pl.BlockSpec(head_blk, lambda b, c: (2 * N_GROUPS + g, q_row_blk0 + b, 0)),
            pl.BlockSpec(memory_space=pl.ANY),
            pl.BlockSpec((nq, nq), lambda b, c: (0, 0)),
        ],
        out_specs=[
            pl.BlockSpec((None, rc * KV_ROWS, LANES), lambda b, c: (b, c, 0)),
            pl.BlockSpec((t_new, d), lambda b, c: (b, 0)),
            pl.BlockSpec((t_new, d), lambda b, c: (b, 0)),
        ],
        out_shape=[
            jax.ShapeDtypeStruct((nb_, w * KV_ROWS, LANES), F32),
            jax.ShapeDtypeStruct((nb_ * t_new, d), F32),
            jax.ShapeDtypeStruct((nb_ * t_new, d), F32),
        ],
        scratch_shapes=[pltpu.VMEM((nc, nq, ncol), F32),
                        pltpu.SemaphoreType.DMA(()),
                        pltpu.VMEM((nq, LANES), F32),
                        pltpu.VMEM((nq, LANES), F32),
                        pltpu.VMEM((nq, HEAD_DIM_A), F32)],
        compiler_params=_params("arbitrary", "arbitrary"),
        name=f"win_sample_w{w}",
    )(cache, cache, qkv_hm, qkv_hm, qkv_hm, bias_past, bias_new)


def _merge_groups_kernel(o0, o1, o2, l0, l1, l2, out_ref):
    ls = [l0[...], l1[...], l2[...]]
    mx = jnp.maximum(jnp.maximum(ls[0], ls[1]), ls[2])
    es = [jnp.exp(x - mx) for x in ls]
    den = es[0] + es[1] + es[2]
    out_ref[...] = (es[0] / den) * o0[...] + (es[1] / den) * o1[...] + (es[2] / den) * o2[...]


def _merge_groups(outs, lses):
    m, d = outs[0].shape
    tm = _pick(m, (256, 128, 64, 32, 16, 8))
    spec = pl.BlockSpec((tm, d), lambda i: (i, 0))
    return pl.pallas_call(
        _merge_groups_kernel,
        grid=(m // tm,),
        in_specs=[spec] * 6,
        out_specs=spec,
        out_shape=jax.ShapeDtypeStruct((m, d), F32),
        compiler_params=_params("arbitrary"),
        name="merge_groups",
    )(*outs, *lses)


def _rms(x, g):
    return x * lax.rsqrt(jnp.mean(x * x, axis=-1, keepdims=True) + RMS_EPS) * g


def _mla_cq_kernel(x_ref, w_ref, g_ref, o_ref, wb_ref):
    @pl.when(pl.program_id(0) == 0)
    def _():
        wb_ref[...] = w_ref[...].astype(BF16)

    acc = jnp.dot(x_ref[...].astype(BF16), wb_ref[...], preferred_element_type=F32)
    o_ref[...] = _rms(acc, g_ref[...])


def _mla_cq(x, w_dq, q_norm):
    m, k = x.shape
    n = w_dq.shape[1]
    tm = _pick(m, (512, 256, 128, 64, 32, 16, 8))
    return pl.pallas_call(
        _mla_cq_kernel,
        grid=(m // tm,),
        in_specs=[pl.BlockSpec((tm, k), lambda i: (i, 0)),
                  pl.BlockSpec((k, n), lambda i: (0, 0)),
                  pl.BlockSpec((1, n), lambda i: (0, 0))],
        out_specs=pl.BlockSpec((tm, n), lambda i: (i, 0)),
        out_shape=jax.ShapeDtypeStruct((m, n), F32),
        scratch_shapes=[pltpu.VMEM((k, n), BF16)],
        compiler_params=_params("arbitrary"),
        name="mla_cq",
    )(x, w_dq, q_norm.reshape(1, n))


def _mla_rows_kernel(x_ref, w_ref, g_ref, cos_ref, sin_ref, o_ref, wb_ref):
    @pl.when(pl.program_id(0) == 0)
    def _():
        wb_ref[...] = w_ref[...].astype(BF16)

    acc = jnp.dot(x_ref[...].astype(BF16), wb_ref[...], preferred_element_type=F32)
    o_ref[:, :KV_LORA] = _rms(acc[:, :KV_LORA], g_ref[...])
    tail = acc[:, KV_LORA:]
    kr = tail[:, :QK_ROPE] * cos_ref[:, :QK_ROPE] + tail[:, QK_ROPE:] * sin_ref[:, :QK_ROPE]
    o_ref[:, KV_LORA:] = kr


def _mla_rows(x, w_ext, kv_norm, cos_t, sin_t):
    m, k = x.shape
    n = w_ext.shape[1]
    tm = _pick(m, (512, 256, 128, 64, 32, 16, 8))
    return pl.pallas_call(
        _mla_rows_kernel,
        grid=(m // tm,),
        in_specs=[pl.BlockSpec((tm, k), lambda i: (i, 0)),
                  pl.BlockSpec((k, n), lambda i: (0, 0)),
                  pl.BlockSpec((1, KV_LORA), lambda i: (0, 0)),
                  pl.BlockSpec((tm, LANES), lambda i: (i, 0)),
                  pl.BlockSpec((tm, LANES), lambda i: (i, 0))],
        out_specs=pl.BlockSpec((tm, MLA_ROW), lambda i: (i, 0)),
        out_shape=jax.ShapeDtypeStruct((m, MLA_ROW), F32),
        scratch_shapes=[pltpu.VMEM((k, n), BF16)],
        compiler_params=_params("arbitrary"),
        name="mla_rows",
    )(x, w_ext, kv_norm.reshape(1, KV_LORA), cos_t, sin_t)


def _mla_q_kernel(cq_ref, w_ref, wuk_ref, cos_ref, sin_ref, o_ref, wb_ref, wukb_ref):
    @pl.when(pl.program_id(0) == 0)
    def _():
        wb_ref[...] = w_ref[...].astype(BF16)
        wukb_ref[...] = wuk_ref[...].astype(BF16)

    hw = N_HEADS_B * LANES
    q = jnp.dot(cq_ref[...].astype(BF16), wb_ref[...], preferred_element_type=F32)
    cos = cos_ref[:, :QK_ROPE]
    sin = sin_ref[:, :QK_ROPE]
    for h in range(N_HEADS_B):
        nope = q[:, h * LANES:(h + 1) * LANES].astype(BF16)
        wuk_h = wukb_ref[:, h * QK_NOPE:(h + 1) * QK_NOPE]
        o_ref[h, :, :KV_LORA] = lax.dot_general(nope, wuk_h, NT_DIMS, preferred_element_type=F32)
        r0 = hw + h * LANES
        o_ref[h, :, KV_LORA:] = (q[:, r0:r0 + QK_ROPE] * cos
                                 + q[:, hw + r0:hw + r0 + QK_ROPE] * sin)


def _mla_q(cq, w_uq_ext, w_uk2, cos_t, sin_t):
    m, k = cq.shape
    n = w_uq_ext.shape[1]
    tm = _pick(m, (256, 128, 64, 32, 16, 8))
    return pl.pallas_call(
        _mla_q_kernel,
        grid=(m // tm,),
        in_specs=[pl.BlockSpec((tm, k), lambda i: (i, 0)),
                  pl.BlockSpec((k, n), lambda i: (0, 0)),
                  pl.BlockSpec(w_uk2.shape, lambda i: (0, 0)),
                  pl.BlockSpec((tm, LANES), lambda i: (i, 0)),
                  pl.BlockSpec((tm, LANES), lambda i: (i, 0))],
        out_specs=pl.BlockSpec((N_HEADS_B, tm, MLA_ROW), lambda i: (0, i, 0)),
        out_shape=jax.ShapeDtypeStruct((N_HEADS_B, m, MLA_ROW), F32),
        scratch_shapes=[pltpu.VMEM((k, n), BF16), pltpu.VMEM(w_uk2.shape, BF16)],
        compiler_params=_params("arbitrary"),
        name="mla_q",
    )(cq, w_uq_ext, w_uk2, cos_t, sin_t)


MLA_KV_TILE = 512


def _mla_prompt_kernel(q_ref, rows_ref, wuv_ref, o_ref, kb_ref, m_ref, l_ref, acc_ref):
    qi = pl.program_id(1)
    nq = N_HEADS_B * QBLK

    @pl.when(qi == 0)
    def _():
        kb_ref[...] = rows_ref[...].astype(BF16)

    q = q_ref[...].reshape(nq, MLA_ROW).astype(BF16)
    m_ref[...] = jnp.full(m_ref.shape, NEG_INF, F32)
    l_ref[...] = jnp.zeros(l_ref.shape, F32)
    acc_ref[...] = jnp.zeros(acc_ref.shape, F32)
    qpos = qi * QBLK + (lax.broadcasted_iota(jnp.int32, (nq, MLA_KV_TILE), 0) & (QBLK - 1))
    kcol = lax.broadcasted_iota(jnp.int32, (nq, MLA_KV_TILE), 1)

    def kv_step(j, carry):
        k = kb_ref[pl.ds(pl.multiple_of(j * MLA_KV_TILE, MLA_KV_TILE), MLA_KV_TILE), :]
        s = lax.dot_general(q, k, NT_DIMS, preferred_element_type=F32) * MLA_SCALE
        s = jnp.where(kcol + j * MLA_KV_TILE <= qpos, s, NEG_INF)
        m_prev = m_ref[...]
        m_new = jnp.maximum(m_prev, jnp.max(s, axis=-1, keepdims=True))
        alpha = jnp.exp(m_prev - m_new)
        p = jnp.exp(s - m_new[:, :1])
        l_ref[...] = alpha * l_ref[...] + jnp.sum(p, axis=-1, keepdims=True)
        acc_ref[...] = alpha[:, :1] * acc_ref[...] + jnp.dot(p.astype(BF16), k[:, :KV_LORA],
                                                             preferred_element_type=F32)
        m_ref[...] = m_new
        return carry

    lax.fori_loop(0, (qi * QBLK) // MLA_KV_TILE + 1, kv_step, 0)

    for h in range(N_HEADS_B):
        rs = slice(h * QBLK, (h + 1) * QBLK)
        o_lat = (acc_ref[rs, :] / l_ref[rs, :1]).astype(BF16)
        o_ref[:, h * V_HEAD:(h + 1) * V_HEAD] = jnp.dot(
            o_lat, wuv_ref[:, h * V_HEAD:(h + 1) * V_HEAD], preferred_element_type=F32).astype(o_ref.dtype)


def _mla_prompt(qcat, rows, wuv_b, batch, seq):
    nqb = seq // QBLK
    nq = N_HEADS_B * QBLK
    return pl.pallas_call(
        _mla_prompt_kernel,
        grid=(batch, nqb),
        in_specs=[pl.BlockSpec((N_HEADS_B, QBLK, MLA_ROW), lambda b, i: (0, b * nqb + i, 0)),
                  pl.BlockSpec((seq, MLA_ROW), lambda b, i: (b, 0)),
                  pl.BlockSpec(wuv_b.shape, lambda b, i: (0, 0))],
        out_specs=pl.BlockSpec((QBLK, N_HEADS_B * V_HEAD), lambda b, i: (b * nqb + i, 0)),
        out_shape=jax.ShapeDtypeStruct((batch * seq, N_HEADS_B * V_HEAD), F32),
        scratch_shapes=[pltpu.VMEM((seq, MLA_ROW), BF16),
                        pltpu.VMEM((nq, LANES), F32),
                        pltpu.VMEM((nq, LANES), F32),
                        pltpu.VMEM((nq, KV_LORA), F32)],
        compiler_params=_params("arbitrary", "arbitrary"),
        name="mla_prompt",
    )(qcat, rows, wuv_b)


MLA_PAGES_PER_STEP = 16


def _mla_sample_kernel(pt_ref, *refs, t_new, n_chunks):
    del pt_ref
    npg = MLA_PAGES_PER_STEP
    q_ref = refs[0]
    pages = refs[1:1 + npg]
    new_ref, wuv_ref, o_ref, m_ref, l_ref, acc_ref = refs[1 + npg:]
    c = pl.program_id(1)
    nq = N_HEADS_B * t_new

    @pl.when(c == 0)
    def _():
        m_ref[...] = jnp.full(m_ref.shape, NEG_INF, F32)
        l_ref[...] = jnp.zeros(l_ref.shape, F32)
        acc_ref[...] = jnp.zeros(acc_ref.shape, F32)

    q = q_ref[...].reshape(nq, MLA_ROW).astype(BF16)

    def update(scores, pv):
        m_prev = m_ref[...]
        m_new = m_prev
        for s in scores:
            m_new = jnp.maximum(m_new, jnp.max(s, axis=-1, keepdims=True))
        alpha = jnp.exp(m_prev - m_new)
        l_new = alpha * l_ref[...]
        acc = alpha[:, :1] * acc_ref[...]
        for i, s in enumerate(scores):
            p = jnp.exp(s - m_new[:, :1])
            l_new = l_new + jnp.sum(p, axis=-1, keepdims=True)
            acc = acc + pv(i, p.astype(BF16))
        m_ref[...] = m_new
        l_ref[...] = l_new
        acc_ref[...] = acc

    keys_t = [p_ref[...].astype(BF16) for p_ref in pages]
    scores = [jnp.dot(q, kt, preferred_element_type=F32) * MLA_SCALE for kt in keys_t]
    update(scores, lambda i, p: lax.dot_general(p, keys_t[i][:KV_LORA, :], NT_DIMS,
                                                preferred_element_type=F32))

    @pl.when(c == n_chunks - 1)
    def _():
        page = pages[0].shape[1]
        k = jnp.concatenate([new_ref[...], jnp.zeros((page - t_new, MLA_ROW), F32)], axis=0).astype(BF16)
        s = lax.dot_general(q, k, NT_DIMS, preferred_element_type=F32) * MLA_SCALE
        tq = lax.broadcasted_iota(jnp.int32, s.shape, 0) % t_new
        col = lax.broadcasted_iota(jnp.int32, s.shape, 1)
        s = jnp.where(col <= tq, s, NEG_INF)
        update([s], lambda i, p: jnp.dot(p, k[:, :KV_LORA], preferred_element_type=F32))
        for h in range(N_HEADS_B):
            rs = slice(h * t_new, (h + 1) * t_new)
            o_lat = (acc_ref[rs, :] / l_ref[rs, :1]).astype(BF16)
            o_ref[:, h * V_HEAD:(h + 1) * V_HEAD] = jnp.dot(
                o_lat, wuv_ref[:, h * V_HEAD:(h + 1) * V_HEAD], preferred_element_type=F32)


def _mla_sample(qcat, q_row_blk0, rows, pool_t, li, page_table, wuv_b, t_new):
    nb_, n_pages = page_table.shape
    page = pool_t.shape[3]
    npg = MLA_PAGES_PER_STEP
    n_chunks = n_pages // npg
    nq = N_HEADS_B * t_new
    page_specs = [
        pl.BlockSpec((None, None, MLA_ROW, page),
                     functools.partial(lambda b, c, pt, k: (li, pt[b, c * npg + k], 0, 0), k=k))
        for k in range(npg)]
    grid_spec = pltpu.PrefetchScalarGridSpec(
        num_scalar_prefetch=1,
        grid=(nb_, n_chunks),
        in_specs=[pl.BlockSpec((N_HEADS_B, t_new, MLA_ROW), lambda b, c, pt: (0, q_row_blk0 + b, 0))]
        + page_specs
        + [pl.BlockSpec((t_new, MLA_ROW), lambda b, c, pt: (q_row_blk0 + b, 0)),
           pl.BlockSpec(wuv_b.shape, lambda b, c, pt: (0, 0))],
        out_specs=pl.BlockSpec((t_new, N_HEADS_B * V_HEAD), lambda b, c, pt: (b, 0)),
        scratch_shapes=[pltpu.VMEM((nq, LANES), F32),
                        pltpu.VMEM((nq, LANES), F32),
                        pltpu.VMEM((nq, KV_LORA), F32)])
    return pl.pallas_call(
        functools.partial(_mla_sample_kernel, t_new=t_new, n_chunks=n_chunks),
        grid_spec=grid_spec,
        out_shape=jax.ShapeDtypeStruct((nb_ * t_new, N_HEADS_B * V_HEAD), F32),
        compiler_params=_params("arbitrary", "arbitrary"),
        name="mla_sample",
    )(page_table, qcat, *([pool_t] * npg), rows, wuv_b)


ROUTE_TM = 256


def _router_kernel(x_ref, w_ref, b_ref, route_ref, gate_ref, cnt_ref, run_ref):
    @pl.when(pl.program_id(0) == 0)
    def _():
        run_ref[...] = jnp.zeros(run_ref.shape, F32)

    x = x_ref[...]
    w = w_ref[...]
    xh = x.astype(BF16)
    xl = (x - xh.astype(F32)).astype(BF16)
    wh = w.astype(BF16)
    wl = (w - wh.astype(F32)).astype(BF16)
    logits = (jnp.dot(xh, wh, preferred_element_type=F32) + jnp.dot(xh, wl, preferred_element_type=F32)
              + jnp.dot(xl, wh, preferred_element_type=F32)) + b_ref[...]
    n_exp = logits.shape[-1]
    lane = lax.broadcasted_iota(jnp.int32, logits.shape, 1).astype(F32)
    out_lane = lax.broadcasted_iota(jnp.int32, route_ref.shape, 1)
    route = jnp.zeros(route_ref.shape, F32)
    val_out = jnp.zeros(gate_ref.shape, F32)
    sel = jnp.zeros(logits.shape, F32)
    top0 = None
    denom = None
    vals, idxs = [], []
    for k in range(TOP_K):
        mx = jnp.max(logits, axis=-1, keepdims=True)
        idx = jnp.min(jnp.where(logits == mx, lane, float(n_exp)), axis=-1, keepdims=True)
        if k == 0:
            top0 = mx
        e = jnp.exp(mx - top0)
        vals.append(e)
        idxs.append(idx)
        denom = e if denom is None else denom + e
        route = jnp.where(out_lane == k, idx, route)
        sel = sel + jnp.where(lane == idx, 1.0, 0.0)
        logits = jnp.where(lane == idx, -jnp.inf, logits)
    for k in range(TOP_K):
        val_out = jnp.where(out_lane == k, vals[k] / denom, val_out)
    tm = x.shape[0]
    before = lax.broadcasted_iota(jnp.int32, (tm, tm), 1) < lax.broadcasted_iota(jnp.int32, (tm, tm), 0)
    rank = jnp.dot(jnp.where(before, 1.0, 0.0).astype(BF16), sel.astype(BF16),
                   preferred_element_type=F32) + run_ref[...]
    for k in range(TOP_K):
        rank_k = jnp.sum(jnp.where(lane == idxs[k], rank, 0.0), axis=-1, keepdims=True)
        route = jnp.where(out_lane == TOP_K + k, rank_k, route)
    run_ref[...] = run_ref[...] + jnp.sum(sel, axis=0, keepdims=True)
    cnt_ref[...] = run_ref[...]
    route_ref[...] = route.astype(jnp.int32)
    gate_ref[...] = val_out


def _router(x, w, b):
    m, d = x.shape
    n_exp = w.shape[1]
    tm = _pick(m, (ROUTE_TM, 128, 64, 32, 16, 8))
    return pl.pallas_call(
        _router_kernel,
        grid=(m // tm,),
        in_specs=[pl.BlockSpec((tm, d), lambda i: (i, 0)),
                  pl.BlockSpec((d, n_exp), lambda i: (0, 0)),
                  pl.BlockSpec((1, n_exp), lambda i: (0, 0))],
        out_specs=[pl.BlockSpec((tm, LANES), lambda i: (i, 0)),
                   pl.BlockSpec((tm, LANES), lambda i: (i, 0)),
                   pl.BlockSpec((1, n_exp), lambda i: (0, 0))],
        out_shape=[jax.ShapeDtypeStruct((m, LANES), jnp.int32),
                   jax.ShapeDtypeStruct((m, LANES), F32),
                   jax.ShapeDtypeStruct((1, n_exp), F32)],
        scratch_shapes=[pltpu.VMEM((1, n_exp), F32)],
        compiler_params=_params("arbitrary"),
        name="router",
    )(x, w, b.reshape(1, n_exp))


def _row_copies(n_rows, make_copy):
    def issue(r, carry):
        for k in range(TOP_K):
            make_copy(r, k).start()
        return carry

    def drain(r, carry):
        for k in range(TOP_K):
            make_copy(r, k).wait()
        return carry

    lax.fori_loop(0, n_rows, issue, 0)
    lax.fori_loop(0, n_rows, drain, 0)


def _dispatch_kernel(dest_ref, x_ref, xs_in, xs_ref, sem, *, tm):
    del xs_in
    base = pl.program_id(0) * (tm * TOP_K)
    _row_copies(tm, lambda r, k: pltpu.make_async_copy(
        x_ref.at[pl.ds(r, 1)], xs_ref.at[pl.ds(dest_ref[base + r * TOP_K + k], 1)], sem))


def _dispatch(dest_flat, x, n_rows):
    n_tok, d = x.shape
    tm = _pick(n_tok, (ROUTE_TM, 128, 64, 32, 16, 8))
    grid_spec = pltpu.PrefetchScalarGridSpec(
        num_scalar_prefetch=1,
        grid=(n_tok // tm,),
        in_specs=[pl.BlockSpec((tm, d), lambda i, dest: (i, 0)),
                  pl.BlockSpec(memory_space=pl.ANY)],
        out_specs=pl.BlockSpec(memory_space=pl.ANY),
        scratch_shapes=[pltpu.SemaphoreType.DMA(())])
    return pl.pallas_call(
        functools.partial(_dispatch_kernel, tm=tm),
        grid_spec=grid_spec,
        out_shape=jax.ShapeDtypeStruct((n_rows, d), x.dtype),
        input_output_aliases={2: 0},
        compiler_params=_params("arbitrary"),
        name="moe_dispatch",
    )(dest_flat, x, jnp.zeros((n_rows, d), x.dtype))


MOE_TM = 512
MOE_TF = 512
MOE_TN = 1024


def _tile_changed(meta_ref, t):
    return jnp.logical_or(t == 0, meta_ref[1 + t] != meta_ref[jnp.maximum(t, 1)])


def _expert_weights(meta_ref, n_tiles, w_hbm, w_buf, wb_ref, sems, run_ref, layer, width):
    col = pl.program_id(0)
    t = pl.program_id(1)
    used = t < meta_ref[0]

    def copies(tile, c, slot):
        e = meta_ref[1 + tile]
        cols = pl.ds(pl.multiple_of(c * width, width), width)
        return [pltpu.make_async_copy(w.at[layer, e, :, cols], buf.at[slot], sems.at[i, slot])
                for i, (w, buf) in enumerate(zip(w_hbm, w_buf))]

    @pl.when(jnp.logical_and(col == 0, t == 0))
    def _():
        run_ref[0] = 0
        for cp in copies(0, 0, 0):
            cp.start()

    @pl.when(jnp.logical_and(used, _tile_changed(meta_ref, t)))
    def _():
        run = run_ref[0]
        slot = run % 2
        for cp in copies(t, col, slot):
            cp.wait()
        nxt = meta_ref[1 + n_tiles + t]

        @pl.when(nxt < meta_ref[0])
        def _():
            for cp in copies(nxt, col, 1 - slot):
                cp.start()

        @pl.when(jnp.logical_and(nxt >= meta_ref[0], col + 1 < pl.num_programs(0)))
        def _():
            for cp in copies(0, col + 1, 1 - slot):
                cp.start()

        for i, buf in enumerate(w_buf):
            wb_ref[i] = buf[slot].astype(BF16)
        run_ref[0] = run + 1

    return used


def _moe_up_kernel(meta_ref, x_ref, wg_hbm, wu_hbm, bg_ref, bu_ref, h_ref,
                   wg_buf, wu_buf, wb_ref, sems, run_ref, *, layer, n_tiles, tf):
    used = _expert_weights(meta_ref, n_tiles, (wg_hbm, wu_hbm), (wg_buf, wu_buf), wb_ref, sems, run_ref,
                           layer, tf)
    wgb_ref = wb_ref.at[0]
    wub_ref = wb_ref.at[1]

    @pl.when(used)
    def _():
        x = x_ref[...].astype(BF16)
        g = jnp.minimum(jnp.dot(x, wgb_ref[...], preferred_element_type=F32) + bg_ref[...], SWIGLU_LIMIT)
        u = jnp.clip(jnp.dot(x, wub_ref[...], preferred_element_type=F32) + bu_ref[...],
                     -SWIGLU_LIMIT, SWIGLU_LIMIT)
        h_ref[...] = ((u + 1.0) * (g * (1.0 / (1.0 + jnp.exp(-SWIGLU_ALPHA * g))))).astype(h_ref.dtype)

    @pl.when(jnp.logical_not(used))
    def _():
        h_ref[...] = jnp.zeros(h_ref.shape, h_ref.dtype)


def _moe_up(meta, xs, w_gate, w_up, b_gate, b_up, layer):
    n_rows, d = xs.shape
    n_exp, dff = b_gate.shape[1:]
    n_tiles = n_rows // MOE_TM
    tf = _pick(dff, (MOE_TF, 256, 128))
    wspec = pl.BlockSpec(memory_space=pl.ANY)
    bspec = pl.BlockSpec((None, None, 1, tf), lambda f, t, mt: (layer, mt[1 + t], 0, f))
    grid_spec = pltpu.PrefetchScalarGridSpec(
        num_scalar_prefetch=1,
        grid=(dff // tf, n_tiles),
        in_specs=[pl.BlockSpec((MOE_TM, d), lambda f, t, mt: (t, 0)), wspec, wspec, bspec, bspec],
        out_specs=pl.BlockSpec((MOE_TM, tf), lambda f, t, mt: (t, f)),
        scratch_shapes=[pltpu.VMEM((2, d, tf), F32), pltpu.VMEM((2, d, tf), F32),
                        pltpu.VMEM((2, d, tf), BF16),
                        pltpu.SemaphoreType.DMA((2, 2)), pltpu.SMEM((1,), jnp.int32)])
    return pl.pallas_call(
        functools.partial(_moe_up_kernel, layer=layer, n_tiles=n_tiles, tf=tf),
        grid_spec=grid_spec,
        out_shape=jax.ShapeDtypeStruct((n_rows, dff), BF16),
        compiler_params=_params("arbitrary", "arbitrary"),
        name="moe_up",
    )(meta, xs, w_gate, w_up, b_gate.reshape(-1, n_exp, 1, dff), b_up.reshape(-1, n_exp, 1, dff))


def _moe_down_kernel(meta_ref, h_ref, wd_hbm, bd_ref, y_ref, wd_buf, wb_ref, sems, run_ref,
                     *, layer, n_tiles, tn):
    used = _expert_weights(meta_ref, n_tiles, (wd_hbm,), (wd_buf,), wb_ref, sems, run_ref, layer, tn)
    wdb_ref = wb_ref.at[0]

    @pl.when(used)
    def _():
        y_ref[...] = jnp.dot(h_ref[...], wdb_ref[...], preferred_element_type=F32) + bd_ref[...]

    @pl.when(jnp.logical_not(used))
    def _():
        y_ref[...] = jnp.zeros(y_ref.shape, y_ref.dtype)


def _moe_down(meta, hid, w_down, b_down, layer):
    n_rows, dff = hid.shape
    n_exp, d = b_down.shape[1:]
    n_tiles = n_rows // MOE_TM
    tn = _pick(d, (MOE_TN, 512, 256, 128))
    grid_spec = pltpu.PrefetchScalarGridSpec(
        num_scalar_prefetch=1,
        grid=(d // tn, n_tiles),
        in_specs=[pl.BlockSpec((MOE_TM, dff), lambda n, t, mt: (t, 0)),
                  pl.BlockSpec(memory_space=pl.ANY),
                  pl.BlockSpec((None, None, 1, tn), lambda n, t, mt: (layer, mt[1 + t], 0, n))],
        out_specs=pl.BlockSpec((MOE_TM, tn), lambda n, t, mt: (t, n)),
        scratch_shapes=[pltpu.VMEM((2, dff, tn), F32), pltpu.VMEM((1, dff, tn), BF16),
                        pltpu.SemaphoreType.DMA((1, 2)), pltpu.SMEM((1,), jnp.int32)])
    return pl.pallas_call(
        functools.partial(_moe_down_kernel, layer=layer, n_tiles=n_tiles, tn=tn),
        grid_spec=grid_spec,
        out_shape=jax.ShapeDtypeStruct((n_rows, d), F32),
        compiler_params=_params("arbitrary", "arbitrary"),
        name="moe_down",
    )(meta, hid, w_down, b_down.reshape(-1, n_exp, 1, d))


def _combine_ln_kernel(dest_ref, x_ref, gate_ref, g_ref, b_ref, yr_ref, o_ref, buf_ref, sem, *, alpha, tm):
    base = pl.program_id(0) * (tm * TOP_K)
    _row_copies(tm, lambda r, k: pltpu.make_async_copy(
        yr_ref.at[pl.ds(dest_ref[base + r * TOP_K + k], 1)], buf_ref.at[k, pl.ds(r, 1)], sem))
    gate = gate_ref[...]
    y = gate[:, 0:1] * buf_ref[0]
    for k in range(1, TOP_K):
        y = y + gate[:, k:k + 1] * buf_ref[k]
    z = alpha * x_ref[...] + y
    mu = jnp.mean(z, axis=-1, keepdims=True)
    zc = z - mu
    var = jnp.mean(zc * zc, axis=-1, keepdims=True)
    o_ref[...] = zc * lax.rsqrt(var + LN_EPS) * g_ref[...] + b_ref[...]


def _combine_ln(dest_flat, x, gate_pad, yr, g, b, alpha):
    n_tok, d = x.shape
    tm = _pick(n_tok, (ROUTE_TM, 128, 64, 32, 16, 8))
    grid_spec = pltpu.PrefetchScalarGridSpec(
        num_scalar_prefetch=1,
        grid=(n_tok // tm,),
        in_specs=[pl.BlockSpec((tm, d), lambda i, dest: (i, 0)),
                  pl.BlockSpec((tm, LANES), lambda i, dest: (i, 0)),
                  pl.BlockSpec((1, d), lambda i, dest: (0, 0)),
                  pl.BlockSpec((1, d), lambda i, dest: (0, 0)),
                  pl.BlockSpec(memory_space=pl.ANY)],
        out_specs=pl.BlockSpec((tm, d), lambda i, dest: (i, 0)),
        scratch_shapes=[pltpu.VMEM((TOP_K, tm, d), F32), pltpu.SemaphoreType.DMA(())])
    return pl.pallas_call(
        functools.partial(_combine_ln_kernel, alpha=alpha, tm=tm),
        grid_spec=grid_spec,
        out_shape=jax.ShapeDtypeStruct((n_tok, d), F32),
        compiler_params=_params("arbitrary"),
        name="moe_combine_ln",
    )(dest_flat, x, gate_pad, g.reshape(1, d), b.reshape(1, d), yr)


def _moe_ln(x, layer, router_w, router_b, w_gate, b_gate, w_up, b_up, w_down, b_down, g, b, alpha):
    n_tok, d = x.shape
    n_exp = router_w.shape[-1]
    route, gate_pad, cnt = _router(x, router_w[layer], router_b[layer])
    counts = cnt[0].astype(jnp.int32)
    padded = (counts + MOE_TM - 1) // MOE_TM * MOE_TM
    pend = jnp.cumsum(padded)
    pstart = pend - padded
    n_tiles = -(-(n_tok * TOP_K + n_exp * (MOE_TM - 1)) // MOE_TM)
    n_rows = n_tiles * MOE_TM
    tile_e = jnp.minimum(jnp.sum(pend[None, :] <= (jnp.arange(n_tiles, dtype=jnp.int32) * MOE_TM)[:, None], axis=1),
                         n_exp - 1)
    n_used = pend[-1] // MOE_TM
    tiles = jnp.arange(n_tiles, dtype=jnp.int32)
    run_start = jnp.concatenate([jnp.ones((1,), bool), tile_e[1:] != tile_e[:-1]]) & (tiles < n_used)
    later = jnp.concatenate([jnp.where(run_start, tiles, n_used)[1:], n_used[None]])
    next_run = jnp.flip(lax.cummin(jnp.flip(later)))
    meta = jnp.concatenate([n_used[None], tile_e, next_run]).astype(jnp.int32)
    top_e = route[:, :TOP_K]
    start = jnp.sum(jnp.where(top_e[:, :, None] == jnp.arange(n_exp, dtype=jnp.int32), pstart, 0), axis=-1)
    dest_flat = (start + route[:, TOP_K:2 * TOP_K]).reshape(-1)
    xs = _dispatch(dest_flat, x, n_rows)
    hid = _moe_up(meta, xs, w_gate, w_up, b_gate, b_up, layer)
    yr = _moe_down(meta, hid, w_down, b_down, layer)
    return _combine_ln(dest_flat, x, gate_pad, yr, g, b, alpha)


def _rope_tables(pos):
    half = QK_ROPE // 2
    inv = ROPE_BASE ** (-jnp.arange(half, dtype=F32) / half)
    ang = pos.astype(F32)[:, None] * inv[None]
    cos, sin = jnp.cos(ang), jnp.sin(ang)
    pad = jnp.zeros((pos.shape[0], LANES - QK_ROPE), F32)
    return (jnp.concatenate([cos, cos, pad], axis=1), jnp.concatenate([sin, sin, pad], axis=1))


def _rot_cols(w):
    half = w.shape[-1] // 2
    return jnp.concatenate([-w[..., half:], w[..., :half]], axis=-1)


def kernel(x_prompt, x_sample, cache_win_w128, cache_win_w512, cache_win_w2048, cache_mla, page_table, rel_bias, w_qkv_a, w_o_a, w_dq, q_norm, w_uq, w_dkv, kv_norm, w_uk, w_uv, w_o_b, ln1_g, ln1_b, ln2_g, ln2_b, router_w, router_b, w_gate, b_gate, w_up, b_up, w_down, b_down):
    batch, seq, d_model = x_prompt.shape
    dec_batch, t_new, _ = x_sample.shape
    depth = ln1_g.shape[0]
    alpha = (2 * depth) ** 0.25
    n_p = batch * seq
    n_s = dec_batch * t_new
    past_len = page_table.shape[1] * cache_mla.shape[2]
    assert t_new == SUBLANES and seq % (QBLK * DIL_GROUPS[-1][1]) == 0 and n_p % t_new == 0
    s_blk0 = n_p // t_new
    win_caches = (cache_win_w128, cache_win_w512, cache_win_w2048)
    hd = N_SLOTS * HEAD_DIM_A

    x = jnp.concatenate([x_prompt.reshape(n_p, d_model), x_sample.reshape(n_s, d_model)], axis=0)
    win_p = [[] for _ in DIL_GROUPS]
    win_s = [[] for _ in DIL_GROUPS]
    mla_p, mla_s = [], []

    for layer in range(depth):
        li = layer // 2
        if layer % 2 == 0:
            qkv_hm = _qkv_proj(x.astype(BF16), w_qkv_a[li])
            bias_mats = []
            qi = jnp.arange(QBLK)[:, None]
            ki = jnp.arange(2 * QBLK)[None, :]
            for g, (window, dil) in enumerate(DIL_GROUPS):
                n_taps = window // dil + 1
                bias_j = _tap_bias(rel_bias, g, dil, n_taps)
                bias_mats.append(_tap_table(bias_j, QBLK + qi - ki, n_taps))
            o_p = _dil_prompt(qkv_hm, bias_mats, batch, seq)
            for g, (window, dil) in enumerate(DIL_GROUPS):
                keep = min(window, seq)
                st = _win_states(qkv_hm, g, batch, seq, keep)
                win_p[g].append(st.reshape(batch, keep, 2, N_SLOTS, HEAD_DIM_A))
            outs, lses = [], []
            same_slot = jnp.eye(N_SLOTS, dtype=bool)
            for g, (window, dil) in enumerate(DIL_GROUPS):
                n_taps = window // dil + 1
                cache = win_caches[g]
                lb = cache.shape[2]
                assert lb == window and lb % t_new == 0
                bias_j = _tap_bias(rel_bias, g, dil, n_taps)
                i_q = jnp.arange(t_new)[:, None]

                def table(dist):
                    tap = jnp.where(dist % dil == 0, dist // dil, -1)
                    return _tap_table(bias_j, tap, n_taps)

                rc = min(lb, 256)
                t_past = table(lb + i_q - jnp.arange(lb)[None, :])
                b_past = jnp.where(same_slot[:, None, None, :], t_past[..., None], NEG_INF)
                b_past = b_past.reshape(N_SLOTS * t_new, lb // rc, rc * N_SLOTS).transpose(1, 0, 2)
                t_nw = table(i_q - jnp.arange(t_new)[None, :])
                b_new = jnp.where(same_slot[:, None, :, None], t_nw[:, :, None, :], NEG_INF)
                b_new = b_new.reshape(N_SLOTS * t_new, N_SLOTS * t_new)
                win_out, o_g, lse_g = _win_sample(cache, li, qkv_hm, s_blk0, g, b_past, b_new, t_new)
                win_s[g].append(win_out.reshape(dec_batch, lb, 2, N_SLOTS, HEAD_DIM_A))
                outs.append(o_g)
                lses.append(lse_g)
            o_s = _merge_groups(outs, lses)
            o_all = jnp.concatenate([o_p, o_s.astype(BF16)], axis=0)
            mix = _matmul(o_all, w_o_a[li])
        else:
            pos = jnp.concatenate([jnp.tile(jnp.arange(seq, dtype=jnp.int32), batch),
                                   jnp.tile(past_len + jnp.arange(t_new, dtype=jnp.int32), dec_batch)])
            cos_t, sin_t = _rope_tables(pos)
            w_kv = w_dkv[li]
            w_kv_ext = jnp.concatenate([w_kv, _rot_cols(w_kv[:, KV_LORA:])], axis=1)
            rows = _mla_rows(x, w_kv_ext, kv_norm[li], cos_t, sin_t)
            mla_p.append(rows[:n_p].reshape(batch, seq, MLA_ROW))
            mla_s.append(rows[n_p:].reshape(dec_batch, t_new, MLA_ROW))
            cq = _mla_cq(x, w_dq[li], q_norm[li])
            wq = w_uq[li].reshape(-1, N_HEADS_B, QK_NOPE + QK_ROPE)
            zpad = jnp.zeros(wq.shape[:2] + (LANES - QK_ROPE,), wq.dtype)
            w_rope = wq[..., QK_NOPE:]
            w_uq_ext = jnp.concatenate([
                wq[..., :QK_NOPE].reshape(wq.shape[0], -1),
                jnp.concatenate([w_rope, zpad], axis=-1).reshape(wq.shape[0], -1),
                jnp.concatenate([_rot_cols(w_rope), zpad], axis=-1).reshape(wq.shape[0], -1)], axis=1)
            qcat = _mla_q(cq, w_uq_ext, w_uk[li].reshape(KV_LORA, N_HEADS_B * QK_NOPE), cos_t, sin_t)
            wuv_b = w_uv[li].reshape(KV_LORA, N_HEADS_B * V_HEAD).astype(BF16)
            v_p = _mla_prompt(qcat, rows, wuv_b, batch, seq)
            v_s = _mla_sample(qcat, s_blk0, rows, jnp.swapaxes(cache_mla, 2, 3), li, page_table, wuv_b, t_new)
            mix = _matmul(jnp.concatenate([v_p, v_s], axis=0), w_o_b[li])
        x = _add_ln(x, mix, ln1_g[layer], ln1_b[layer], alpha)
        x = _moe_ln(x, layer, router_w, router_b, w_gate, b_gate, w_up, b_up, w_down, b_down,
                    ln2_g[layer], ln2_b[layer], alpha)

    xp = x[:n_p].reshape(batch, seq, d_model)
    xs = x[n_p:].reshape(dec_batch, t_new, d_model)
    return (xp, xs,
            jnp.stack(win_p[0], 0), jnp.stack(win_p[1], 0), jnp.stack(win_p[2], 0), jnp.stack(mla_p, 0),
            jnp.stack(win_s[0], 0), jnp.stack(win_s[1], 0), jnp.stack(win_s[2], 0), jnp.stack(mla_s, 0))
```

```python
import functools
import math

import jax
import jax.numpy as jnp
from jax import lax
from jax.experimental import pallas as pl
from jax.experimental.pallas import tpu as pltpu

F32 = jnp.float32
BF16 = jnp.bfloat16

DIL_GROUPS = ((128, 1), (512, 4), (2048, 16))
N_GROUPS = len(DIL_GROUPS)
N_SLOTS = 16
HEAD_DIM_A = 128
QBLK = 128
A_SCALE = HEAD_DIM_A ** -0.5
N_BUCKETS = 32
MAX_DISTANCE = 2048
N_HEADS_B = 16
KV_LORA = 512
QK_NOPE = 128
QK_ROPE = 64
V_HEAD = 128
ROPE_BASE = 10000.0
MLA_SCALE = (QK_NOPE + QK_ROPE) ** -0.5
MLA_ROW = KV_LORA + QK_ROPE
TOP_K = 4
SWIGLU_ALPHA = 1.702
SWIGLU_LIMIT = 7.0
LN_EPS = 1e-5
RMS_EPS = 1e-6
NEG_INF = -1e30

LANES = 128
SUBLANES = 8
VMEM_LIMIT = 56 * 1024 * 1024
KV_ROWS = 2 * N_SLOTS
QKV_TN = 1024

NT_DIMS = (((1,), (1,)), ((), ()))


def _pick(n, candidates):
    for c in candidates:
        if n % c == 0:
            return c
    raise ValueError(f"no tile in {candidates} divides {n}")


def _params(*sem):
    return pltpu.CompilerParams(dimension_semantics=sem, vmem_limit_bytes=VMEM_LIMIT)


def _mm_kernel(x_ref, w_ref, o_ref, wb_ref):
    @pl.when(pl.program_id(1) == 0)
    def _():
        wb_ref[...] = w_ref[...].astype(BF16)

    o_ref[...] = jnp.dot(x_ref[...].astype(BF16), wb_ref[...],
                         preferred_element_type=F32).astype(o_ref.dtype)


def _matmul(x, w, out_dtype=F32):
    m, k = x.shape
    n = w.shape[1]
    tm = _pick(m, (512, 256, 128, 64, 32, 16, 8))
    tn = _pick(n, (512, 256, 128))
    return pl.pallas_call(
        _mm_kernel,
        grid=(n // tn, m // tm),
        in_specs=[pl.BlockSpec((tm, k), lambda j, i: (i, 0)),
                  pl.BlockSpec((k, tn), lambda j, i: (0, j))],
        out_specs=pl.BlockSpec((tm, tn), lambda j, i: (i, j)),
        out_shape=jax.ShapeDtypeStruct((m, n), out_dtype),
        scratch_shapes=[pltpu.VMEM((k, tn), BF16)],
        compiler_params=_params("arbitrary", "arbitrary"),
        name="matmul",
    )(x, w)


def _qkv_kernel(x_ref, w_ref, hm_ref, wb_ref, *, heads_per_tile):
    @pl.when(pl.program_id(1) == 0)
    def _():
        wb_ref[...] = w_ref[...].astype(BF16)

    acc = jnp.dot(x_ref[...].astype(BF16), wb_ref[...], preferred_element_type=F32)
    for j in range(heads_per_tile):
        hm_ref[j] = acc[:, j * HEAD_DIM_A:(j + 1) * HEAD_DIM_A]


def _qkv_proj(x, w):
    m, k = x.shape
    n = w.shape[1]
    tm = _pick(m, (512, 256, 128, 64, 32, 16, 8))
    tn = QKV_TN
    hpt = tn // HEAD_DIM_A
    return pl.pallas_call(
        functools.partial(_qkv_kernel, heads_per_tile=hpt),
        grid=(n // tn, m // tm),
        in_specs=[pl.BlockSpec((tm, k), lambda j, i: (i, 0)),
                  pl.BlockSpec((k, tn), lambda j, i: (0, j))],
        out_specs=pl.BlockSpec((hpt, tm, HEAD_DIM_A), lambda j, i: (j, i, 0)),
        out_shape=jax.ShapeDtypeStruct((n // HEAD_DIM_A, m, HEAD_DIM_A), F32),
        scratch_shapes=[pltpu.VMEM((k, tn), BF16)],
        compiler_params=_params("arbitrary", "arbitrary"),
        name="qkv_proj",
    )(x, w)


def _win_states_kernel(k_ref, v_ref, o_ref, *, rows):
    for h in range(N_SLOTS):
        o_ref[pl.ds(h, rows, stride=KV_ROWS), :] = k_ref[h]
        o_ref[pl.ds(N_SLOTS + h, rows, stride=KV_ROWS), :] = v_ref[h]


def _win_states(qkv_hm, g, batch, seq, keep):
    rows = min(keep, 256)
    nck = keep // rows
    first = (seq - keep) // rows
    per_seq = seq // rows
    kblk = N_GROUPS + g
    vblk = 2 * N_GROUPS + g
    return pl.pallas_call(
        functools.partial(_win_states_kernel, rows=rows),
        grid=(batch, nck),
        in_specs=[pl.BlockSpec((N_SLOTS, rows, HEAD_DIM_A), lambda b, c: (kblk, b * per_seq + first + c, 0)),
                  pl.BlockSpec((N_SLOTS, rows, HEAD_DIM_A), lambda b, c: (vblk, b * per_seq + first + c, 0))],
        out_specs=pl.BlockSpec((None, rows * KV_ROWS, LANES), lambda b, c: (b, c, 0)),
        out_shape=jax.ShapeDtypeStruct((batch, keep * KV_ROWS, LANES), F32),
        compiler_params=_params("arbitrary", "arbitrary"),
        name=f"win_states_{keep}",
    )(qkv_hm, qkv_hm)


def _add_ln_kernel(x_ref, y_ref, g_ref, b_ref, o_ref, *, alpha):
    z = alpha * x_ref[...] + y_ref[...].astype(F32)
    mu = jnp.mean(z, axis=-1, keepdims=True)
    zc = z - mu
    var = jnp.mean(zc * zc, axis=-1, keepdims=True)
    o_ref[...] = zc * lax.rsqrt(var + LN_EPS) * g_ref[...] + b_ref[...]


def _add_ln(x, y, g, b, alpha):
    m, d = x.shape
    tm = _pick(m, (512, 256, 128, 64, 32, 16, 8))
    return pl.pallas_call(
        functools.partial(_add_ln_kernel, alpha=alpha),
        grid=(m // tm,),
        in_specs=[pl.BlockSpec((tm, d), lambda i: (i, 0)),
                  pl.BlockSpec((tm, d), lambda i: (i, 0)),
                  pl.BlockSpec((1, d), lambda i: (0, 0)),
                  pl.BlockSpec((1, d), lambda i: (0, 0))],
        out_specs=pl.BlockSpec((tm, d), lambda i: (i, 0)),
        out_shape=jax.ShapeDtypeStruct((m, d), F32),
        compiler_params=_params("arbitrary"),
        name="add_ln",
    )(x, y, g.reshape(1, d), b.reshape(1, d))


def _t5_bucket(dist):
    max_exact = N_BUCKETS // 2
    d = jnp.maximum(dist, 1).astype(F32)
    large = max_exact + (jnp.log(d / max_exact) / math.log(MAX_DISTANCE / max_exact)
                         * (N_BUCKETS - max_exact)).astype(jnp.int32)
    large = jnp.minimum(large, N_BUCKETS - 1)
    return jnp.where(dist < max_exact, dist, large)


def _tap_bias(rel_bias, g, dil, n_taps):
    b = rel_bias[_t5_bucket(jnp.arange(n_taps, dtype=jnp.int32) * dil)]
    return b[:, g * N_SLOTS:(g + 1) * N_SLOTS].T.astype(F32)


def _tap_table(bias_j, tap, n_taps):
    ok = (tap >= 0) & (tap < n_taps)
    onehot = (tap.reshape(1, -1) == jnp.arange(n_taps, dtype=tap.dtype)[:, None]).astype(F32)
    t = jnp.dot(bias_j, onehot, precision=lax.Precision.HIGHEST).reshape(bias_j.shape[0], *tap.shape)
    return jnp.where(ok[None], t, NEG_INF)


def _dil_prompt_kernel(*refs, seq):
    qkv = refs[:9]
    bias = refs[9:12]
    o_ref = refs[12]
    o_scr, lse_scr = refs[13], refs[14]

    def blk(t, g, dil):
        q_ref, k_ref, v_ref = qkv[3 * g], qkv[3 * g + 1], qkv[3 * g + 2]
        nb = seq // dil // QBLK
        r = t // nb
        n = t % nb
        start = r + n * (QBLK * dil)
        pstart = r + jnp.maximum(n - 1, 0) * (QBLK * dil)
        cur = pl.ds(start, QBLK, stride=dil) if dil > 1 else pl.ds(start, QBLK)
        prv = pl.ds(pstart, QBLK, stride=dil) if dil > 1 else pl.ds(pstart, QBLK)
        q = q_ref[cur, :].astype(BF16)
        kk = jnp.concatenate([k_ref[prv, :], k_ref[cur, :]], axis=0).astype(BF16)
        vv = jnp.concatenate([v_ref[prv, :], v_ref[cur, :]], axis=0).astype(BF16)
        s = lax.dot_general(q, kk, NT_DIMS, preferred_element_type=F32) * A_SCALE + bias[g][...]
        ki = lax.broadcasted_iota(jnp.int32, s.shape, 1)
        s = jnp.where((ki >= QBLK) | (n > 0), s, NEG_INF)
        m = jnp.max(s, axis=-1, keepdims=True)
        p = jnp.exp(s - m)
        l = jnp.sum(p, axis=-1, keepdims=True)
        o = jnp.dot(p.astype(BF16), vv, preferred_element_type=F32) / l
        o_scr[g, cur, :] = o
        lse_scr[g, cur, :] = jnp.broadcast_to(m + jnp.log(l), (QBLK, HEAD_DIM_A))

    def blocks(t, carry):
        for g, (_, dil) in enumerate(DIL_GROUPS):
            blk(t, g, dil)
        return carry

    lax.fori_loop(0, seq // QBLK, blocks, 0)

    rows = 256

    def merge(c, carry):
        sl = pl.ds(pl.multiple_of(c * rows, rows), rows)
        ls = [lse_scr[g, sl, :] for g in range(N_GROUPS)]
        mx = jnp.maximum(jnp.maximum(ls[0], ls[1]), ls[2])
        es = [jnp.exp(x - mx) for x in ls]
        den = es[0] + es[1] + es[2]
        out = (es[0] / den) * o_scr[0, sl, :] + (es[1] / den) * o_scr[1, sl, :] + (es[2] / den) * o_scr[2, sl, :]
        o_ref[sl, :] = out.astype(o_ref.dtype)
        return carry

    lax.fori_loop(0, seq // rows, merge, 0)


def _dil_prompt(qkv_hm, bias_mats, batch, seq):
    gh = N_GROUPS * N_SLOTS
    in_specs = []
    for g in range(N_GROUPS):
        for which in range(3):
            in_specs.append(pl.BlockSpec(
                (None, seq, HEAD_DIM_A),
                functools.partial(lambda b, h, base: (base + h, b, 0), base=which * gh + g * N_SLOTS)))
    for g in range(N_GROUPS):
        in_specs.append(pl.BlockSpec((None, QBLK, 2 * QBLK), lambda b, h: (h, 0, 0)))
    return pl.pallas_call(
        functools.partial(_dil_prompt_kernel, seq=seq),
        grid=(batch, N_SLOTS),
        in_specs=in_specs,
        out_specs=pl.BlockSpec((seq, HEAD_DIM_A), lambda b, h: (b, h)),
        out_shape=jax.ShapeDtypeStruct((batch * seq, N_SLOTS * HEAD_DIM_A), BF16),
        scratch_shapes=[pltpu.VMEM((N_GROUPS, seq, HEAD_DIM_A), F32),
                        pltpu.VMEM((N_GROUPS, seq, HEAD_DIM_A), F32)],
        compiler_params=_params("arbitrary", "arbitrary"),
        name="dil_prompt",
    )(*([qkv_hm] * 9), *bias_mats)


def _win_sample_kernel(cache_ref, next_ref, q_ref, kn_ref, vn_ref, bt_hbm, btn_ref,
                       win_ref, o_ref, lse_ref, bt_ref, sem, m_ref, l_ref, acc_ref, *, rc, nc, t_new):
    b = pl.program_id(0)
    c = pl.program_id(1)
    keep = (rc - t_new) * KV_ROWS
    nq = N_SLOTS * t_new

    @pl.when((b == 0) & (c == 0))
    def _():
        cp = pltpu.make_async_copy(bt_hbm, bt_ref, sem)
        cp.start()
        cp.wait()

    @pl.when(c == 0)
    def _():
        m_ref[...] = jnp.full(m_ref.shape, NEG_INF, F32)
        l_ref[...] = jnp.zeros(l_ref.shape, F32)
        acc_ref[...] = jnp.zeros(acc_ref.shape, F32)

    win_ref[0:keep, :] = cache_ref[t_new:].reshape(keep, LANES)

    @pl.when(c < nc - 1)
    def _():
        win_ref[keep:, :] = next_ref[...].reshape(t_new * KV_ROWS, LANES)

    q = q_ref[...].reshape(nq, HEAD_DIM_A).astype(BF16)

    def step(k, v, bias):
        s = lax.dot_general(q, k, NT_DIMS, preferred_element_type=F32) * A_SCALE + bias
        m_prev = m_ref[...]
        m_new = jnp.maximum(m_prev, jnp.max(s, axis=-1, keepdims=True))
        alpha = jnp.exp(m_prev - m_new)
        p = jnp.exp(s - m_new[:, :1])
        l_ref[...] = alpha * l_ref[...] + jnp.sum(p, axis=-1, keepdims=True)
        acc_ref[...] = alpha * acc_ref[...] + jnp.dot(p.astype(BF16), v, preferred_element_type=F32)
        m_ref[...] = m_new

    step(cache_ref[:, 0].reshape(rc * N_SLOTS, HEAD_DIM_A).astype(BF16),
         cache_ref[:, 1].reshape(rc * N_SLOTS, HEAD_DIM_A).astype(BF16), bt_ref[c])

    @pl.when(c == nc - 1)
    def _():
        for h in range(N_SLOTS):
            win_ref[pl.ds(keep + h, t_new, stride=KV_ROWS), :] = kn_ref[h]
            win_ref[pl.ds(keep + N_SLOTS + h, t_new, stride=KV_ROWS), :] = vn_ref[h]
        step(kn_ref[...].reshape(nq, HEAD_DIM_A).astype(BF16),
             vn_ref[...].reshape(nq, HEAD_DIM_A).astype(BF16), btn_ref[...])
        o = acc_ref[...] / l_ref[...]
        lse = m_ref[...] + jnp.log(l_ref[...])
        for h in range(N_SLOTS):
            sl = slice(h * HEAD_DIM_A, (h + 1) * HEAD_DIM_A)
            o_ref[:, sl] = o[h * t_new:(h + 1) * t_new]
            lse_ref[:, sl] = lse[h * t_new:(h + 1) * t_new]


def _win_sample(cache, li, qkv_hm, q_row_blk0, g, bias_past, bias_new, t_new):
    nb_, w = cache.shape[1], cache.shape[2]
    nc, nq, ncol = bias_past.shape
    rc = w // nc
    d = N_SLOTS * HEAD_DIM_A
    kern = functools.partial(_win_sample_kernel, rc=rc, nc=nc, t_new=t_new)
    head_blk = (N_SLOTS, t_new, HEAD_DIM_A)
    return pl.pallas_call(
        kern,
        grid=(nb_, nc),
        in_specs=[
            pl.BlockSpec((None, None, rc, 2, N_SLOTS, HEAD_DIM_A), lambda b, c: (li, b, c, 0, 0, 0)),
            pl.BlockSpec((None, None, t_new, 2, N_SLOTS, HEAD_DIM_A),
                         lambda b, c: (li, b, jnp.minimum((c + 1) * (rc // t_new), w // t_new - 1), 0, 0, 0)),
            pl.BlockSpec(head_blk, lambda b, c: (g, q_row_blk0 + b, 0)),
            pl.BlockSpec(head_blk, lambda b, c: (N_GROUPS + g, q_row_blk0 + b, 0)),
            pl.BlockSpec(head_blk, lambda b, c: (2 * N_GROUPS + g, q_row_blk0 + b, 0)),
            pl.BlockSpec(memory_space=pl.ANY),
            pl.BlockSpec((nq, nq), lambda b, c: (0, 0)),
        ],
        out_specs=[
            pl.BlockSpec((None, rc * KV_ROWS, LANES), lambda b, c: (b, c, 0)),
            pl.BlockSpec((t_new, d), lambda b, c: (b, 0)),
            pl.BlockSpec((t_new, d), lambda b, c: (b, 0)),
        ],
        out_shape=[
            jax.ShapeDtypeStruct((nb_, w * KV_ROWS, LANES), F32),
            jax.ShapeDtypeStruct((nb_ * t_new, d), F32),
            jax.ShapeDtypeStruct((nb_ * t_new, d), F32),
        ],
        scratch_shapes=[pltpu.VMEM((nc, nq, ncol), F32),
                        pltpu.SemaphoreType.DMA(()),
                        pltpu.VMEM((nq, LANES), F32),
                        pltpu.VMEM((nq, LANES), F32),
                        pltpu.VMEM((nq, HEAD_DIM_A), F32)],
        compiler_params=_params("arbitrary", "arbitrary"),
        name=f"win_sample_w{w}",
    )(cache, cache, qkv_hm, qkv_hm, qkv_hm, bias_past, bias_new)


def _merge_groups_kernel(o0, o1, o2, l0, l1, l2, out_ref):
    ls = [l0[...], l1[...], l2[...]]
    mx = jnp.maximum(jnp.maximum(ls[0], ls[1]), ls[2])
    es = [jnp.exp(x - mx) for x in ls]
    den = es[0] + es[1] + es[2]
    out_ref[...] = (es[0] / den) * o0[...] + (es[1] / den) * o1[...] + (es[2] / den) * o2[...]


def _merge_groups(outs, lses):
    m, d = outs[0].shape
    tm = _pick(m, (256, 128, 64, 32, 16, 8))
    spec = pl.BlockSpec((tm, d), lambda i: (i, 0))
    return pl.pallas_call(
        _merge_groups_kernel,
        grid=(m // tm,),
        in_specs=[spec] * 6,
        out_specs=spec,
        out_shape=jax.ShapeDtypeStruct((m, d), F32),
        compiler_params=_params("arbitrary"),
        name="merge_groups",
    )(*outs, *lses)


def _rms(x, g):
    return x * lax.rsqrt(jnp.mean(x * x, axis=-1, keepdims=True) + RMS_EPS) * g


def _mla_cq_kernel(x_ref, w_ref, g_ref, o_ref, wb_ref):
    @pl.when(pl.program_id(0) == 0)
    def _():
        wb_ref[...] = w_ref[...].astype(BF16)

    acc = jnp.dot(x_ref[...].astype(BF16), wb_ref[...], preferred_element_type=F32)
    o_ref[...] = _rms(acc, g_ref[...])


def _mla_cq(x, w_dq, q_norm):
    m, k = x.shape
    n = w_dq.shape[1]
    tm = _pick(m, (512, 256, 128, 64, 32, 16, 8))
    return pl.pallas_call(
        _mla_cq_kernel,
        grid=(m // tm,),
        in_specs=[pl.BlockSpec((tm, k), lambda i: (i, 0)),
                  pl.BlockSpec((k, n), lambda i: (0, 0)),
                  pl.BlockSpec((1, n), lambda i: (0, 0))],
        out_specs=pl.BlockSpec((tm, n), lambda i: (i, 0)),
        out_shape=jax.ShapeDtypeStruct((m, n), F32),
        scratch_shapes=[pltpu.VMEM((k, n), BF16)],
        compiler_params=_params("arbitrary"),
        name="mla_cq",
    )(x, w_dq, q_norm.reshape(1, n))


def _mla_rows_kernel(x_ref, w_ref, g_ref, cos_ref, sin_ref, o_ref, wb_ref):
    @pl.when(pl.program_id(0) == 0)
    def _():
        wb_ref[...] = w_ref[...].astype(BF16)

    acc = jnp.dot(x_ref[...].astype(BF16), wb_ref[...], preferred_element_type=F32)
    o_ref[:, :KV_LORA] = _rms(acc[:, :KV_LORA], g_ref[...])
    tail = acc[:, KV_LORA:]
    kr = tail[:, :QK_ROPE] * cos_ref[:, :QK_ROPE] + tail[:, QK_ROPE:] * sin_ref[:, :QK_ROPE]
    o_ref[:, KV_LORA:] = kr


def _mla_rows(x, w_ext, kv_norm, cos_t, sin_t):
    m, k = x.shape
    n = w_ext.shape[1]
    tm = _pick(m, (512, 256, 128, 64, 32, 16, 8))
    return pl.pallas_call(
        _mla_rows_kernel,
        grid=(m // tm,),
        in_specs=[pl.BlockSpec((tm, k), lambda i: (i, 0)),
                  pl.BlockSpec((k, n), lambda i: (0, 0)),
                  pl.BlockSpec((1, KV_LORA), lambda i: (0, 0)),
                  pl.BlockSpec((tm, LANES), lambda i: (i, 0)),
                  pl.BlockSpec((tm, LANES), lambda i: (i, 0))],
        out_specs=pl.BlockSpec((tm, MLA_ROW), lambda i: (i, 0)),
        out_shape=jax.ShapeDtypeStruct((m, MLA_ROW), F32),
        scratch_shapes=[pltpu.VMEM((k, n), BF16)],
        compiler_params=_params("arbitrary"),
        name="mla_rows",
    )(x, w_ext, kv_norm.reshape(1, KV_LORA), cos_t, sin_t)


def _mla_q_kernel(cq_ref, w_ref, wuk_ref, cos_ref, sin_ref, o_ref, wb_ref, wukb_ref):
    @pl.when(pl.program_id(0) == 0)
    def _():
        wb_ref[...] = w_ref[...].astype(BF16)
        wukb_ref[...] = wuk_ref[...].astype(BF16)

    hw = N_HEADS_B * LANES
    q = jnp.dot(cq_ref[...].astype(BF16), wb_ref[...], preferred_element_type=F32)
    cos = cos_ref[:, :QK_ROPE]
    sin = sin_ref[:, :QK_ROPE]
    for h in range(N_HEADS_B):
        nope = q[:, h * LANES:(h + 1) * LANES].astype(BF16)
        wuk_h = wukb_ref[:, h * QK_NOPE:(h + 1) * QK_NOPE]
        o_ref[h, :, :KV_LORA] = lax.dot_general(nope, wuk_h, NT_DIMS, preferred_element_type=F32)
        r0 = hw + h * LANES
        o_ref[h, :, KV_LORA:] = (q[:, r0:r0 + QK_ROPE] * cos
                                 + q[:, hw + r0:hw + r0 + QK_ROPE] * sin)


def _mla_q(cq, w_uq_ext, w_uk2, cos_t, sin_t):
    m, k = cq.shape
    n = w_uq_ext.shape[1]
    tm = _pick(m, (256, 128, 64, 32, 16, 8))
    return pl.pallas_call(
        _mla_q_kernel,
        grid=(m // tm,),
        in_specs=[pl.BlockSpec((tm, k), lambda i: (i, 0)),
                  pl.BlockSpec((k, n), lambda i: (0, 0)),
                  pl.BlockSpec(w_uk2.shape, lambda i: (0, 0)),
                  pl.BlockSpec((tm, LANES), lambda i: (i, 0)),
                  pl.BlockSpec((tm, LANES), lambda i: (i, 0))],
        out_specs=pl.BlockSpec((N_HEADS_B, tm, MLA_ROW), lambda i: (0, i, 0)),
        out_shape=jax.ShapeDtypeStruct((N_HEADS_B, m, MLA_ROW), F32),
        scratch_shapes=[pltpu.VMEM((k, n), BF16), pltpu.VMEM(w_uk2.shape, BF16)],
        compiler_params=_params("arbitrary"),
        name="mla_q",
    )(cq, w_uq_ext, w_uk2, cos_t, sin_t)


MLA_KV_TILE = 512


def _mla_prompt_kernel(q_ref, rows_ref, wuv_ref, o_ref, kb_ref, m_ref, l_ref, acc_ref):
    qi = pl.program_id(1)
    nq = N_HEADS_B * QBLK

    @pl.when(qi == 0)
    def _():
        kb_ref[...] = rows_ref[...].astype(BF16)

    q = q_ref[...].reshape(nq, MLA_ROW).astype(BF16)
    m_ref[...] = jnp.full(m_ref.shape, NEG_INF, F32)
    l_ref[...] = jnp.zeros(l_ref.shape, F32)
    acc_ref[...] = jnp.zeros(acc_ref.shape, F32)
    qpos = qi * QBLK + (lax.broadcasted_iota(jnp.int32, (nq, MLA_KV_TILE), 0) & (QBLK - 1))
    kcol = lax.broadcasted_iota(jnp.int32, (nq, MLA_KV_TILE), 1)

    def kv_step(j, carry):
        k = kb_ref[pl.ds(pl.multiple_of(j * MLA_KV_TILE, MLA_KV_TILE), MLA_KV_TILE), :]
        s = lax.dot_general(q, k, NT_DIMS, preferred_element_type=F32) * MLA_SCALE
        s = jnp.where(kcol + j * MLA_KV_TILE <= qpos, s, NEG_INF)
        m_prev = m_ref[...]
        m_new = jnp.maximum(m_prev, jnp.max(s, axis=-1, keepdims=True))
        alpha = jnp.exp(m_prev - m_new)
        p = jnp.exp(s - m_new[:, :1])
        l_ref[...] = alpha * l_ref[...] + jnp.sum(p, axis=-1, keepdims=True)
        acc_ref[...] = alpha[:, :1] * acc_ref[...] + jnp.dot(p.astype(BF16), k[:, :KV_LORA],
                                                             preferred_element_type=F32)
        m_ref[...] = m_new
        return carry

    lax.fori_loop(0, (qi * QBLK) // MLA_KV_TILE + 1, kv_step, 0)

    for h in range(N_HEADS_B):
        rs = slice(h * QBLK, (h + 1) * QBLK)
        o_lat = (acc_ref[rs, :] / l_ref[rs, :1]).astype(BF16)
        o_ref[:, h * V_HEAD:(h + 1) * V_HEAD] = jnp.dot(
            o_lat, wuv_ref[:, h * V_HEAD:(h + 1) * V_HEAD], preferred_element_type=F32).astype(o_ref.dtype)


def _mla_prompt(qcat, rows, wuv_b, batch, seq):
    nqb = seq // QBLK
    nq = N_HEADS_B * QBLK
    return pl.pallas_call(
        _mla_prompt_kernel,
        grid=(batch, nqb),
        in_specs=[pl.BlockSpec((N_HEADS_B, QBLK, MLA_ROW), lambda b, i: (0, b * nqb + i, 0)),
                  pl.BlockSpec((seq, MLA_ROW), lambda b, i: (b, 0)),
                  pl.BlockSpec(wuv_b.shape, lambda b, i: (0, 0))],
        out_specs=pl.BlockSpec((QBLK, N_HEADS_B * V_HEAD), lambda b, i: (b * nqb + i, 0)),
        out_shape=jax.ShapeDtypeStruct((batch * seq, N_HEADS_B * V_HEAD), F32),
        scratch_shapes=[pltpu.VMEM((seq, MLA_ROW), BF16),
                        pltpu.VMEM((nq, LANES), F32),
                        pltpu.VMEM((nq, LANES), F32),
                        pltpu.VMEM((nq, KV_LORA), F32)],
        compiler_params=_params("arbitrary", "arbitrary"),
        name="mla_prompt",
    )(qcat, rows, wuv_b)


MLA_PAGES_PER_STEP = 16


def _mla_sample_kernel(pt_ref, *refs, t_new, n_chunks):
    del pt_ref
    npg = MLA_PAGES_PER_STEP
    q_ref = refs[0]
    pages = refs[1:1 + npg]
    new_ref, wuv_ref, o_ref, m_ref, l_ref, acc_ref = refs[1 + npg:]
    c = pl.program_id(1)
    nq = N_HEADS_B * t_new

    @pl.when(c == 0)
    def _():
        m_ref[...] = jnp.full(m_ref.shape, NEG_INF, F32)
        l_ref[...] = jnp.zeros(l_ref.shape, F32)
        acc_ref[...] = jnp.zeros(acc_ref.shape, F32)

    q = q_ref[...].reshape(nq, MLA_ROW).astype(BF16)

    def update(scores, pv):
        m_prev = m_ref[...]
        m_new = m_prev
        for s in scores:
            m_new = jnp.maximum(m_new, jnp.max(s, axis=-1, keepdims=True))
        alpha = jnp.exp(m_prev - m_new)
        l_new = alpha * l_ref[...]
        acc = alpha[:, :1] * acc_ref[...]
        for i, s in enumerate(scores):
            p = jnp.exp(s - m_new[:, :1])
            l_new = l_new + jnp.sum(p, axis=-1, keepdims=True)
            acc = acc + pv(i, p.astype(BF16))
        m_ref[...] = m_new
        l_ref[...] = l_new
        acc_ref[...] = acc

    keys_t = [p_ref[...].astype(BF16) for p_ref in pages]
    scores = [jnp.dot(q, kt, preferred_element_type=F32) * MLA_SCALE for kt in keys_t]
    update(scores, lambda i, p: lax.dot_general(p, keys_t[i][:KV_LORA, :], NT_DIMS,
                                                preferred_element_type=F32))

    @pl.when(c == n_chunks - 1)
    def _():
        page = pages[0].shape[1]
        k = jnp.concatenate([new_ref[...], jnp.zeros((page - t_new, MLA_ROW), F32)], axis=0).astype(BF16)
        s = lax.dot_general(q, k, NT_DIMS, preferred_element_type=F32) * MLA_SCALE
        tq = lax.broadcasted_iota(jnp.int32, s.shape, 0) % t_new
        col = lax.broadcasted_iota(jnp.int32, s.shape, 1)
        s = jnp.where(col <= tq, s, NEG_INF)
        update([s], lambda i, p: jnp.dot(p, k[:, :KV_LORA], preferred_element_type=F32))
        for h in range(N_HEADS_B):
            rs = slice(h * t_new, (h + 1) * t_new)
            o_lat = (acc_ref[rs, :] / l_ref[rs, :1]).astype(BF16)
            o_ref[:, h * V_HEAD:(h + 1) * V_HEAD] = jnp.dot(
                o_lat, wuv_ref[:, h * V_HEAD:(h + 1) * V_HEAD], preferred_element_type=F32)


def _mla_sample(qcat, q_row_blk0, rows, pool_t, li, page_table, wuv_b, t_new):
    nb_, n_pages = page_table.shape
    page = pool_t.shape[3]
    npg = MLA_PAGES_PER_STEP
    n_chunks = n_pages // npg
    nq = N_HEADS_B * t_new
    page_specs = [
        pl.BlockSpec((None, None, MLA_ROW, page),
                     functools.partial(lambda b, c, pt, k: (li, pt[b, c * npg + k], 0, 0), k=k))
        for k in range(npg)]
    grid_spec = pltpu.PrefetchScalarGridSpec(
        num_scalar_prefetch=1,
        grid=(nb_, n_chunks),
        in_specs=[pl.BlockSpec((N_HEADS_B, t_new, MLA_ROW), lambda b, c, pt: (0, q_row_blk0 + b, 0))]
        + page_specs
        + [pl.BlockSpec((t_new, MLA_ROW), lambda b, c, pt: (q_row_blk0 + b, 0)),
           pl.BlockSpec(wuv_b.shape, lambda b, c, pt: (0, 0))],
        out_specs=pl.BlockSpec((t_new, N_HEADS_B * V_HEAD), lambda b, c, pt: (b, 0)),
        scratch_shapes=[pltpu.VMEM((nq, LANES), F32),
                        pltpu.VMEM((nq, LANES), F32),
                        pltpu.VMEM((nq, KV_LORA), F32)])
    return pl.pallas_call(
        functools.partial(_mla_sample_kernel, t_new=t_new, n_chunks=n_chunks),
        grid_spec=grid_spec,
        out_shape=jax.ShapeDtypeStruct((nb_ * t_new, N_HEADS_B * V_HEAD), F32),
        compiler_params=_params("arbitrary", "arbitrary"),
        name="mla_sample",
    )(page_table, qcat, *([pool_t] * npg), rows, wuv_b)


ROUTE_TM = 256


def _router_kernel(x_ref, w_ref, b_ref, route_ref, gate_ref, cnt_ref, run_ref):
    @pl.when(pl.program_id(0) == 0)
    def _():
        run_ref[...] = jnp.zeros(run_ref.shape, F32)

    x = x_ref[...]
    w = w_ref[...]
    xh = x.astype(BF16)
    xl = (x - xh.astype(F32)).astype(BF16)
    wh = w.astype(BF16)
    wl = (w - wh.astype(F32)).astype(BF16)
    logits = (jnp.dot(xh, wh, preferred_element_type=F32) + jnp.dot(xh, wl, preferred_element_type=F32)
              + jnp.dot(xl, wh, preferred_element_type=F32)) + b_ref[...]
    n_exp = logits.shape[-1]
    lane = lax.broadcasted_iota(jnp.int32, logits.shape, 1).astype(F32)
    out_lane = lax.broadcasted_iota(jnp.int32, route_ref.shape, 1)
    route = jnp.zeros(route_ref.shape, F32)
    val_out = jnp.zeros(gate_ref.shape, F32)
    sel = jnp.zeros(logits.shape, F32)
    top0 = None
    denom = None
    vals, idxs = [], []
    for k in range(TOP_K):
        mx = jnp.max(logits, axis=-1, keepdims=True)
        idx = jnp.min(jnp.where(logits == mx, lane, float(n_exp)), axis=-1, keepdims=True)
        if k == 0:
            top0 = mx
        e = jnp.exp(mx - top0)
        vals.append(e)
        idxs.append(idx)
        denom = e if denom is None else denom + e
        route = jnp.where(out_lane == k, idx, route)
        sel = sel + jnp.where(lane == idx, 1.0, 0.0)
        logits = jnp.where(lane == idx, -jnp.inf, logits)
    for k in range(TOP_K):
        val_out = jnp.where(out_lane == k, vals[k] / denom, val_out)
    tm = x.shape[0]
    before = lax.broadcasted_iota(jnp.int32, (tm, tm), 1) < lax.broadcasted_iota(jnp.int32, (tm, tm), 0)
    rank = jnp.dot(jnp.where(before, 1.0, 0.0).astype(BF16), sel.astype(BF16),
                   preferred_element_type=F32) + run_ref[...]
    for k in range(TOP_K):
        rank_k = jnp.sum(jnp.where(lane == idxs[k], rank, 0.0), axis=-1, keepdims=True)
        route = jnp.where(out_lane == TOP_K + k, rank_k, route)
    run_ref[...] = run_ref[...] + jnp.sum(sel, axis=0, keepdims=True)
    cnt_ref[...] = run_ref[...]
    route_ref[...] = route.astype(jnp.int32)
    gate_ref[...] = val_out


def _router(x, w, b):
    m, d = x.shape
    n_exp = w.shape[1]
    tm = _pick(m, (ROUTE_TM, 128, 64, 32, 16, 8))
    return pl.pallas_call(
        _router_kernel,
        grid=(m // tm,),
        in_specs=[pl.BlockSpec((tm, d), lambda i: (i, 0)),
                  pl.BlockSpec((d, n_exp), lambda i: (0, 0)),
                  pl.BlockSpec((1, n_exp), lambda i: (0, 0))],
        out_specs=[pl.BlockSpec((tm, LANES), lambda i: (i, 0)),
                   pl.BlockSpec((tm, LANES), lambda i: (i, 0)),
                   pl.BlockSpec((1, n_exp), lambda i: (0, 0))],
        out_shape=[jax.ShapeDtypeStruct((m, LANES), jnp.int32),
                   jax.ShapeDtypeStruct((m, LANES), F32),
                   jax.ShapeDtypeStruct((1, n_exp), F32)],
        scratch_shapes=[pltpu.VMEM((1, n_exp), F32)],
        compiler_params=_params("arbitrary"),
        name="router",
    )(x, w, b.reshape(1, n_exp))


def _row_copies(n_rows, make_copy):
    def issue(r, carry):
        for k in range(TOP_K):
            make_copy(r, k).start()
        return carry

    def drain(r, carry):
        for k in range(TOP_K):
            make_copy(r, k).wait()
        return carry

    lax.fori_loop(0, n_rows, issue, 0)
    lax.fori_loop(0, n_rows, drain, 0)


HI16 = 0xFFFF0000


def _pack_bf16_pairs(x):
    half = x.shape[1] // 2
    bits = lax.bitcast_convert_type(x.astype(BF16).astype(F32), jnp.uint32)
    return lax.shift_right_logical(bits[:, :half], jnp.uint32(16)) | (bits[:, half:] & jnp.uint32(HI16))


def _unpack_bf16_pairs(p):
    lo = lax.bitcast_convert_type(lax.shift_left(p, jnp.uint32(16)), F32).astype(BF16)
    hi = lax.bitcast_convert_type(p & jnp.uint32(HI16), F32).astype(BF16)
    return jnp.concatenate([lo, hi], axis=1)


def _dispatch_kernel(dest_ref, x_ref, xs_in, xs_ref, pk_ref, sem, *, tm):
    del xs_in
    base = pl.program_id(0) * (tm * TOP_K)
    pk_ref[...] = _pack_bf16_pairs(x_ref[...])
    _row_copies(tm, lambda r, k: pltpu.make_async_copy(
        pk_ref.at[pl.ds(r, 1)], xs_ref.at[pl.ds(dest_ref[base + r * TOP_K + k], 1)], sem))


def _dispatch(dest_flat, x, n_rows):
    n_tok, d = x.shape
    tm = _pick(n_tok, (ROUTE_TM, 128, 64, 32, 16, 8))
    grid_spec = pltpu.PrefetchScalarGridSpec(
        num_scalar_prefetch=1,
        grid=(n_tok // tm,),
        in_specs=[pl.BlockSpec((tm, d), lambda i, dest: (i, 0)),
                  pl.BlockSpec(memory_space=pl.ANY)],
        out_specs=pl.BlockSpec(memory_space=pl.ANY),
        scratch_shapes=[pltpu.VMEM((tm, d // 2), jnp.uint32), pltpu.SemaphoreType.DMA(())])
    return pl.pallas_call(
        functools.partial(_dispatch_kernel, tm=tm),
        grid_spec=grid_spec,
        out_shape=jax.ShapeDtypeStruct((n_rows, d // 2), jnp.uint32),
        input_output_aliases={2: 0},
        compiler_params=_params("arbitrary"),
        name="moe_dispatch",
    )(dest_flat, x, jnp.zeros((n_rows, d // 2), jnp.uint32))


MOE_TM = 512
MOE_TF = 512
MOE_TN = 1024


def _tile_changed(meta_ref, t):
    return jnp.logical_or(t == 0, meta_ref[1 + t] != meta_ref[jnp.maximum(t, 1)])


def _expert_weights(meta_ref, n_tiles, w_hbm, w_buf, wb_ref, sems, run_ref, layer, width):
    col = pl.program_id(0)
    t = pl.program_id(1)
    used = t < meta_ref[0]

    def copies(tile, c, slot):
        e = meta_ref[1 + tile]
        cols = pl.ds(pl.multiple_of(c * width, width), width)
        return [pltpu.make_async_copy(w.at[layer, e, :, cols], buf.at[slot], sems.at[i, slot])
                for i, (w, buf) in enumerate(zip(w_hbm, w_buf))]

    @pl.when(jnp.logical_and(col == 0, t == 0))
    def _():
        run_ref[0] = 0
        for cp in copies(0, 0, 0):
            cp.start()

    @pl.when(jnp.logical_and(used, _tile_changed(meta_ref, t)))
    def _():
        run = run_ref[0]
        slot = run % 2
        for cp in copies(t, col, slot):
            cp.wait()
        nxt = meta_ref[1 + n_tiles + t]

        @pl.when(nxt < meta_ref[0])
        def _():
            for cp in copies(nxt, col, 1 - slot):
                cp.start()

        @pl.when(jnp.logical_and(nxt >= meta_ref[0], col + 1 < pl.num_programs(0)))
        def _():
            for cp in copies(0, col + 1, 1 - slot):
                cp.start()

        for i, buf in enumerate(w_buf):
            wb_ref[i] = buf[slot].astype(BF16)
        run_ref[0] = run + 1

    return used


def _moe_up_kernel(meta_ref, x_ref, wg_hbm, wu_hbm, bg_ref, bu_ref, h_ref,
                   wg_buf, wu_buf, wb_ref, sems, run_ref, *, layer, n_tiles, tf):
    used = _expert_weights(meta_ref, n_tiles, (wg_hbm, wu_hbm), (wg_buf, wu_buf), wb_ref, sems, run_ref,
                           layer, tf)
    wgb_ref = wb_ref.at[0]
    wub_ref = wb_ref.at[1]

    @pl.when(used)
    def _():
        x = _unpack_bf16_pairs(x_ref[...])
        g = jnp.minimum(jnp.dot(x, wgb_ref[...], preferred_element_type=F32) + bg_ref[...], SWIGLU_LIMIT)
        u = jnp.clip(jnp.dot(x, wub_ref[...], preferred_element_type=F32) + bu_ref[...],
                     -SWIGLU_LIMIT, SWIGLU_LIMIT)
        h_ref[...] = ((u + 1.0) * (g * (1.0 / (1.0 + jnp.exp(-SWIGLU_ALPHA * g))))).astype(h_ref.dtype)

    @pl.when(jnp.logical_not(used))
    def _():
        h_ref[...] = jnp.zeros(h_ref.shape, h_ref.dtype)


def _moe_up(meta, xs, w_gate, w_up, b_gate, b_up, layer):
    n_rows, dhalf = xs.shape
    d = 2 * dhalf
    n_exp, dff = b_gate.shape[1:]
    n_tiles = n_rows // MOE_TM
    tf = _pick(dff, (MOE_TF, 256, 128))
    wspec = pl.BlockSpec(memory_space=pl.ANY)
    bspec = pl.BlockSpec((None, None, 1, tf), lambda f, t, mt: (layer, mt[1 + t], 0, f))
    grid_spec = pltpu.PrefetchScalarGridSpec(
        num_scalar_prefetch=1,
        grid=(dff // tf, n_tiles),
        in_specs=[pl.BlockSpec((MOE_TM, dhalf), lambda f, t, mt: (t, 0)), wspec, wspec, bspec, bspec],
        out_specs=pl.BlockSpec((MOE_TM, tf), lambda f, t, mt: (t, f)),
        scratch_shapes=[pltpu.VMEM((2, d, tf), F32), pltpu.VMEM((2, d, tf), F32),
                        pltpu.VMEM((2, d, tf), BF16),
                        pltpu.SemaphoreType.DMA((2, 2)), pltpu.SMEM((1,), jnp.int32)])
    return pl.pallas_call(
        functools.partial(_moe_up_kernel, layer=layer, n_tiles=n_tiles, tf=tf),
        grid_spec=grid_spec,
        out_shape=jax.ShapeDtypeStruct((n_rows, dff), BF16),
        compiler_params=_params("arbitrary", "arbitrary"),
        name="moe_up",
    )(meta, xs, w_gate, w_up, b_gate.reshape(-1, n_exp, 1, dff), b_up.reshape(-1, n_exp, 1, dff))


def _moe_down_kernel(meta_ref, h_ref, wd_hbm, bd_ref, y_ref, wd_buf, wb_ref, sems, run_ref,
                     *, layer, n_tiles, tn):
    used = _expert_weights(meta_ref, n_tiles, (wd_hbm,), (wd_buf,), wb_ref, sems, run_ref, layer, tn)
    wdb_ref = wb_ref.at[0]

    @pl.when(used)
    def _():
        y_ref[...] = jnp.dot(h_ref[...], wdb_ref[...], preferred_element_type=F32) + bd_ref[...]

    @pl.when(jnp.logical_not(used))
    def _():
        y_ref[...] = jnp.zeros(y_ref.shape, y_ref.dtype)


def _moe_down(meta, hid, w_down, b_down, layer):
    n_rows, dff = hid.shape
    n_exp, d = b_down.shape[1:]
    n_tiles = n_rows // MOE_TM
    tn = _pick(d, (MOE_TN, 512, 256, 128))
    grid_spec = pltpu.PrefetchScalarGridSpec(
        num_scalar_prefetch=1,
        grid=(d // tn, n_tiles),
        in_specs=[pl.BlockSpec((MOE_TM, dff), lambda n, t, mt: (t, 0)),
                  pl.BlockSpec(memory_space=pl.ANY),
                  pl.BlockSpec((None, None, 1, tn), lambda n, t, mt: (layer, mt[1 + t], 0, n))],
        out_specs=pl.BlockSpec((MOE_TM, tn), lambda n, t, mt: (t, n)),
        scratch_shapes=[pltpu.VMEM((2, dff, tn), F32), pltpu.VMEM((1, dff, tn), BF16),
                        pltpu.SemaphoreType.DMA((1, 2)), pltpu.SMEM((1,), jnp.int32)])
    return pl.pallas_call(
        functools.partial(_moe_down_kernel, layer=layer, n_tiles=n_tiles, tn=tn),
        grid_spec=grid_spec,
        out_shape=jax.ShapeDtypeStruct((n_rows, d), F32),
        compiler_params=_params("arbitrary", "arbitrary"),
        name="moe_down",
    )(meta, hid, w_down, b_down.reshape(-1, n_exp, 1, d))


def _combine_ln_kernel(dest_ref, x_ref, gate_ref, g_ref, b_ref, yr_ref, o_ref, buf_ref, sem, *, alpha, tm):
    base = pl.program_id(0) * (tm * TOP_K)
    _row_copies(tm, lambda r, k: pltpu.make_async_copy(
        yr_ref.at[pl.ds(dest_ref[base + r * TOP_K + k], 1)], buf_ref.at[k, pl.ds(r, 1)], sem))
    gate = gate_ref[...]
    y = gate[:, 0:1] * buf_ref[0]
    for k in range(1, TOP_K):
        y = y + gate[:, k:k + 1] * buf_ref[k]
    z = alpha * x_ref[...] + y
    mu = jnp.mean(z, axis=-1, keepdims=True)
    zc = z - mu
    var = jnp.mean(zc * zc, axis=-1, keepdims=True)
    o_ref[...] = zc * lax.rsqrt(var + LN_EPS) * g_ref[...] + b_ref[...]


def _combine_ln(dest_flat, x, gate_pad, yr, g, b, alpha):
    n_tok, d = x.shape
    tm = _pick(n_tok, (ROUTE_TM, 128, 64, 32, 16, 8))
    grid_spec = pltpu.PrefetchScalarGridSpec(
        num_scalar_prefetch=1,
        grid=(n_tok // tm,),
        in_specs=[pl.BlockSpec((tm, d), lambda i, dest: (i, 0)),
                  pl.BlockSpec((tm, LANES), lambda i, dest: (i, 0)),
                  pl.BlockSpec((1, d), lambda i, dest: (0, 0)),
                  pl.BlockSpec((1, d), lambda i, dest: (0, 0)),
                  pl.BlockSpec(memory_space=pl.ANY)],
        out_specs=pl.BlockSpec((tm, d), lambda i, dest: (i, 0)),
        scratch_shapes=[pltpu.VMEM((TOP_K, tm, d), F32), pltpu.SemaphoreType.DMA(())])
    return pl.pallas_call(
        functools.partial(_combine_ln_kernel, alpha=alpha, tm=tm),
        grid_spec=grid_spec,
        out_shape=jax.ShapeDtypeStruct((n_tok, d), F32),
        compiler_params=_params("arbitrary"),
        name="moe_combine_ln",
    )(dest_flat, x, gate_pad, g.reshape(1, d), b.reshape(1, d), yr)


def _moe_ln(x, layer, router_w, router_b, w_gate, b_gate, w_up, b_up, w_down, b_down, g, b, alpha):
    n_tok, d = x.shape
    n_exp = router_w.shape[-1]
    route, gate_pad, cnt = _router(x, router_w[layer], router_b[layer])
    counts = cnt[0].astype(jnp.int32)
    padded = (counts + MOE_TM - 1) // MOE_TM * MOE_TM
    pend = jnp.cumsum(padded)
    pstart = pend - padded
    n_tiles = -(-(n_tok * TOP_K + n_exp * (MOE_TM - 1)) // MOE_TM)
    n_rows = n_tiles * MOE_TM
    tile_e = jnp.minimum(jnp.sum(pend[None, :] <= (jnp.arange(n_tiles, dtype=jnp.int32) * MOE_TM)[:, None], axis=1),
                         n_exp - 1)
    n_used = pend[-1] // MOE_TM
    tiles = jnp.arange(n_tiles, dtype=jnp.int32)
    run_start = jnp.concatenate([jnp.ones((1,), bool), tile_e[1:] != tile_e[:-1]]) & (tiles < n_used)
    later = jnp.concatenate([jnp.where(run_start, tiles, n_used)[1:], n_used[None]])
    next_run = jnp.flip(lax.cummin(jnp.flip(later)))
    meta = jnp.concatenate([n_used[None], tile_e, next_run]).astype(jnp.int32)
    top_e = route[:, :TOP_K]
    start = jnp.sum(jnp.where(top_e[:, :, None] == jnp.arange(n_exp, dtype=jnp.int32), pstart, 0), axis=-1)
    dest_flat = (start + route[:, TOP_K:2 * TOP_K]).reshape(-1)
    xs = _dispatch(dest_flat, x, n_rows)
    hid = _moe_up(meta, xs, w_gate, w_up, b_gate, b_up, layer)
    yr = _moe_down(meta, hid, w_down, b_down, layer)
    return _combine_ln(dest_flat, x, gate_pad, yr, g, b, alpha)


def _rope_tables(pos):
    half = QK_ROPE // 2
    inv = ROPE_BASE ** (-jnp.arange(half, dtype=F32) / half)
    ang = pos.astype(F32)[:, None] * inv[None]
    cos, sin = jnp.cos(ang), jnp.sin(ang)
    pad = jnp.zeros((pos.shape[0], LANES - QK_ROPE), F32)
    return (jnp.concatenate([cos, cos, pad], axis=1), jnp.concatenate([sin, sin, pad], axis=1))


def _rot_cols(w):
    half = w.shape[-1] // 2
    return jnp.concatenate([-w[..., half:], w[..., :half]], axis=-1)


def kernel(x_prompt, x_sample, cache_win_w128, cache_win_w512, cache_win_w2048, cache_mla, page_table, rel_bias, w_qkv_a, w_o_a, w_dq, q_norm, w_uq, w_dkv, kv_norm, w_uk, w_uv, w_o_b, ln1_g, ln1_b, ln2_g, ln2_b, router_w, router_b, w_gate, b_gate, w_up, b_up, w_down, b_down):
    batch, seq, d_model = x_prompt.shape
    dec_batch, t_new, _ = x_sample.shape
    depth = ln1_g.shape[0]
    alpha = (2 * depth) ** 0.25
    n_p = batch * seq
    n_s = dec_batch * t_new
    past_len = page_table.shape[1] * cache_mla.shape[2]
    assert t_new == SUBLANES and seq % (QBLK * DIL_GROUPS[-1][1]) == 0 and n_p % t_new == 0
    s_blk0 = n_p // t_new
    win_caches = (cache_win_w128, cache_win_w512, cache_win_w2048)
    hd = N_SLOTS * HEAD_DIM_A

    x = jnp.concatenate([x_prompt.reshape(n_p, d_model), x_sample.reshape(n_s, d_model)], axis=0)
    win_p = [[] for _ in DIL_GROUPS]
    win_s = [[] for _ in DIL_GROUPS]
    mla_p, mla_s = [], []

    for layer in range(depth):
        li = layer // 2
        if layer % 2 == 0:
            qkv_hm = _qkv_proj(x.astype(BF16), w_qkv_a[li])
            bias_mats = []
            qi = jnp.arange(QBLK)[:, None]
            ki = jnp.arange(2 * QBLK)[None, :]
            for g, (window, dil) in enumerate(DIL_GROUPS):
                n_taps = window // dil + 1
                bias_j = _tap_bias(rel_bias, g, dil, n_taps)
                bias_mats.append(_tap_table(bias_j, QBLK + qi - ki, n_taps))
            o_p = _dil_prompt(qkv_hm, bias_mats, batch, seq)
            for g, (window, dil) in enumerate(DIL_GROUPS):
                keep = min(window, seq)
                st = _win_states(qkv_hm, g, batch, seq, keep)
                win_p[g].append(st.reshape(batch, keep, 2, N_SLOTS, HEAD_DIM_A))
            outs, lses = [], []
            same_slot = jnp.eye(N_SLOTS, dtype=bool)
            for g, (window, dil) in enumerate(DIL_GROUPS):
                n_taps = window // dil + 1
                cache = win_caches[g]
                lb = cache.shape[2]
                assert lb == window and lb % t_new == 0
                bias_j = _tap_bias(rel_bias, g, dil, n_taps)
                i_q = jnp.arange(t_new)[:, None]

                def table(dist):
                    tap = jnp.where(dist % dil == 0, dist // dil, -1)
                    return _tap_table(bias_j, tap, n_taps)

                rc = min(lb, 256)
                t_past = table(lb + i_q - jnp.arange(lb)[None, :])
                t_past = t_past.reshape(N_SLOTS, t_new, lb // rc, rc).transpose(2, 0, 1, 3)
                b_past = jnp.where(same_slot[None, :, None, None, :], t_past[..., None], NEG_INF)
                b_past = b_past.reshape(lb // rc, N_SLOTS * t_new, rc * N_SLOTS)
                t_nw = table(i_q - jnp.arange(t_new)[None, :])
                b_new = jnp.where(same_slot[:, None, :, None], t_nw[:, :, None, :], NEG_INF)
                b_new = b_new.reshape(N_SLOTS * t_new, N_SLOTS * t_new)
                win_out, o_g, lse_g = _win_sample(cache, li, qkv_hm, s_blk0, g, b_past, b_new, t_new)
                win_s[g].append(win_out.reshape(dec_batch, lb, 2, N_SLOTS, HEAD_DIM_A))
                outs.append(o_g)
                lses.append(lse_g)
            o_s = _merge_groups(outs, lses)
            o_all = jnp.concatenate([o_p, o_s.astype(BF16)], axis=0)
            mix = _matmul(o_all, w_o_a[li])
        else:
            pos = jnp.concatenate([jnp.tile(jnp.arange(seq, dtype=jnp.int32), batch),
                                   jnp.tile(past_len + jnp.arange(t_new, dtype=jnp.int32), dec_batch)])
            cos_t, sin_t = _rope_tables(pos)
            w_kv = w_dkv[li]
            w_kv_ext = jnp.concatenate([w_kv, _rot_cols(w_kv[:, KV_LORA:])], axis=1)
            rows = _mla_rows(x, w_kv_ext, kv_norm[li], cos_t, sin_t)
            mla_p.append(rows[:n_p].reshape(batch, seq, MLA_ROW))
            mla_s.append(rows[n_p:].reshape(dec_batch, t_new, MLA_ROW))
            cq = _mla_cq(x, w_dq[li], q_norm[li])
            wq = w_uq[li].reshape(-1, N_HEADS_B, QK_NOPE + QK_ROPE)
            zpad = jnp.zeros(wq.shape[:2] + (LANES - QK_ROPE,), wq.dtype)
            w_rope = wq[..., QK_NOPE:]
            w_uq_ext = jnp.concatenate([
                wq[..., :QK_NOPE].reshape(wq.shape[0], -1),
                jnp.concatenate([w_rope, zpad], axis=-1).reshape(wq.shape[0], -1),
                jnp.concatenate([_rot_cols(w_rope), zpad], axis=-1).reshape(wq.shape[0], -1)], axis=1)
            qcat = _mla_q(cq, w_uq_ext, w_uk[li].reshape(KV_LORA, N_HEADS_B * QK_NOPE), cos_t, sin_t)
            wuv_b = w_uv[li].reshape(KV_LORA, N_HEADS_B * V_HEAD).astype(BF16)
            v_p = _mla_prompt(qcat, rows, wuv_b, batch, seq)
            v_s = _mla_sample(qcat, s_blk0, rows, jnp.swapaxes(cache_mla, 2, 3), li, page_table, wuv_b, t_new)
            mix = _matmul(jnp.concatenate([v_p, v_s], axis=0), w_o_b[li])
        x = _add_ln(x, mix, ln1_g[layer], ln1_b[layer], alpha)
        x = _moe_ln(x, layer, router_w, router_b, w_gate, b_gate, w_up, b_up, w_down, b_down,
                    ln2_g[layer], ln2_b[layer], alpha)

    xp = x[:n_p].reshape(batch, seq, d_model)
    xs = x[n_p:].reshape(dec_batch, t_new, d_model)
    return (xp, xs,
            jnp.stack(win_p[0], 0), jnp.stack(win_p[1], 0), jnp.stack(win_p[2], 0), jnp.stack(mla_p, 0),
            jnp.stack(win_s[0], 0), jnp.stack(win_s[1], 0), jnp.stack(win_s[2], 0), jnp.stack(mla_s, 0))
```
